```python
import math
import jax, jax.numpy as jnp
from jax import lax
import numpy as np

D_MODEL = 4096
BATCH = 4
SEQ = 2048
DEPTH = 4

CHUNK = 64
Q_BLOCK = 128
N_MIXERS = 3
HEAD_DIM = 128
DIFF_HEADS = D_MODEL // (2 * HEAD_DIM)
SB_HEADS = D_MODEL // HEAD_DIM
POOL_GROUPS = 4
POOL_WINDOWS = (2, 4, 8, 16)
POOL_WIDTH = D_MODEL // POOL_GROUPS
N_EXPERTS = 16
N_GROUPS = 4
EXPERTS_PER_GROUP = N_EXPERTS // N_GROUPS
TOP_K = 2
D_FF_EXPERT = 3 * D_MODEL // 16
ROPE_THETA = 10000.0
LN_EPS = 1e-5
RMS_EPS = 1e-5

kernel_name = 'hybrid_diffattn_pool_stickbreak_groupmoe'


def layer_norm(x, g, b):
    xf = x.astype(jnp.float32)
    mu = jnp.mean(xf, axis=-1, keepdims=True)
    var = jnp.mean(jnp.square(xf - mu), axis=-1, keepdims=True)
    return ((xf - mu) * lax.rsqrt(var + LN_EPS) * g + b).astype(x.dtype)


def apply_rope(t, positions):
    half = HEAD_DIM // 2
    inv_freq = ROPE_THETA ** (-jnp.arange(half, dtype=jnp.float32) / half)
    ang = positions.astype(jnp.float32)[:, None] * inv_freq[None, :]
    cos, sin = jnp.cos(ang), jnp.sin(ang)
    tf = t.astype(jnp.float32)
    t1, t2 = tf[..., :half], tf[..., half:]
    return jnp.concatenate([t1 * cos - t2 * sin, t2 * cos + t1 * sin], axis=-1).astype(t.dtype)


def diff_attention(x, w_qkv, w_o, lq1, lk1, lq2, lk2, subln_g, lambda_init):
    B, S, _ = x.shape
    q, k, v = jnp.split(x @ w_qkv, 3, axis=-1)
    q = q.reshape(B, S, 2 * DIFF_HEADS, HEAD_DIM).transpose(0, 2, 1, 3)
    k = k.reshape(B, S, 2 * DIFF_HEADS, HEAD_DIM).transpose(0, 2, 1, 3)
    v = v.reshape(B, S, DIFF_HEADS, 2 * HEAD_DIM).transpose(0, 2, 1, 3)
    pos = jnp.arange(S)
    q, k = apply_rope(q, pos), apply_rope(k, pos)
    lam = (jnp.exp(jnp.sum(lq1.astype(jnp.float32) * lk1.astype(jnp.float32)))
           - jnp.exp(jnp.sum(lq2.astype(jnp.float32) * lk2.astype(jnp.float32))) + lambda_init)
    scale = HEAD_DIM ** -0.5
    outs = []
    for q0 in range(0, S, Q_BLOCK):
        kl = q0 + Q_BLOCK
        s = jnp.einsum('bhqd,bhkd->bhqk', q[:, :, q0:kl], k[:, :, :kl]).astype(jnp.float32) * scale
        q_chunk = (q0 + jnp.arange(Q_BLOCK)) // CHUNK
        k_chunk = jnp.arange(kl) // CHUNK
        mask = k_chunk[None, :] <= q_chunk[:, None]
        p = jax.nn.softmax(jnp.where(mask, s, -jnp.inf), axis=-1)
        p = p.reshape(B, DIFF_HEADS, 2, Q_BLOCK, kl)
        a = p[:, :, 0] - lam * p[:, :, 1]
        outs.append(jnp.einsum('bhqk,bhkd->bhqd', a.astype(v.dtype), v[:, :, :kl]))
    o = jnp.concatenate(outs, axis=2).astype(jnp.float32)
    o = o * lax.rsqrt(jnp.mean(jnp.square(o), axis=-1, keepdims=True) + RMS_EPS) * subln_g
    o = (o * (1.0 - lambda_init)).transpose(0, 2, 1, 3).reshape(B, S, D_MODEL).astype(x.dtype)
    return o @ w_o


def pool_mixer(x, w_pool, pool_scale):
    B, S, _ = x.shape
    xg = x.reshape(B, S, POOL_GROUPS, POOL_WIDTH)
    cs = jnp.cumsum(xg.astype(jnp.float32), axis=1)
    t = jnp.arange(S)
    outs = []
    for g, w in enumerate(POOL_WINDOWS):
        c = cs[:, :, g]
        lagged = jnp.pad(c, ((0, 0), (w, 0), (0, 0)))[:, :S]
        cnt = jnp.minimum(t + 1, w).astype(jnp.float32)[None, :, None]
        outs.append((c - lagged) / cnt - xg[:, :, g].astype(jnp.float32))
    pooled = jnp.stack(outs, axis=2).astype(x.dtype)
    y = jnp.einsum('bsgc,gcd->bsgd', pooled, w_pool).reshape(B, S, D_MODEL)
    return y * pool_scale


def stick_breaking_attention(x, w_qkv, w_o):
    B, S, _ = x.shape
    q, k, v = jnp.split(x @ w_qkv, 3, axis=-1)
    q = q.reshape(B, S, SB_HEADS, HEAD_DIM).transpose(0, 2, 1, 3)
    k = k.reshape(B, S, SB_HEADS, HEAD_DIM).transpose(0, 2, 1, 3)
    v = v.reshape(B, S, SB_HEADS, HEAD_DIM).transpose(0, 2, 1, 3)
    scale = HEAD_DIM ** -0.5
    outs = []
    for q0 in range(0, S, Q_BLOCK):
        kl = q0 + Q_BLOCK
        z = jnp.einsum('bhqd,bhkd->bhqk', q[:, :, q0:kl], k[:, :, :kl]).astype(jnp.float32) * scale
        q_pos = q0 + jnp.arange(Q_BLOCK)
        k_pos = jnp.arange(kl)
        strict = k_pos[None, :] < q_pos[:, None]
        log_keep = jnp.where(strict, jax.nn.log_sigmoid(-z), 0.0)
        later = lax.cumsum(log_keep, axis=3, reverse=True) - log_keep
        weights = jnp.where(strict, jnp.exp(jax.nn.log_sigmoid(z) + later), 0.0)
        outs.append(jnp.einsum('bhqk,bhkd->bhqd', weights.astype(v.dtype), v[:, :, :kl]))
    o = jnp.concatenate(outs, axis=2).transpose(0, 2, 1, 3).reshape(B, S, D_MODEL)
    return o @ w_o


def grouped_moe(x, router_w, router_bias, w_gate, w_up, w_down):
    B, S, D = x.shape
    xt = x.reshape(-1, D)
    aff = jax.nn.sigmoid((xt @ router_w).astype(jnp.float32))
    sel = (aff + router_bias.astype(jnp.float32)).reshape(-1, N_GROUPS, EXPERTS_PER_GROUP)
    group_score = lax.top_k(sel, TOP_K)[0].sum(-1)
    g = jnp.argmax(group_score, axis=-1)
    in_group = jnp.take_along_axis(sel, g[:, None, None], axis=1)[:, 0]
    _, loc = lax.top_k(in_group, TOP_K)
    idx = g[:, None] * EXPERTS_PER_GROUP + loc
    gates = jnp.take_along_axis(aff, idx, axis=-1)
    gates = gates / jnp.sum(gates, axis=-1, keepdims=True)
    combine = jnp.sum(jax.nn.one_hot(idx, N_EXPERTS, dtype=jnp.float32) * gates[..., None], axis=1)
    h = jax.nn.silu(jnp.einsum('td,edf->tef', xt, w_gate)) * jnp.einsum('td,edf->tef', xt, w_up)
    h = h * combine[:, :, None].astype(h.dtype)
    return jnp.einsum('tef,efd->td', h, w_down).reshape(B, S, D)


def setup_inputs(seed: int = 0) -> dict:
    key = jax.random.key(seed)
    ks = jax.random.split(key, 24)
    f32 = jnp.float32
    D = D_MODEL
    n_a = len(range(0, DEPTH, N_MIXERS))
    n_b = len(range(1, DEPTH, N_MIXERS))
    n_c = len(range(2, DEPTH, N_MIXERS))
    beta = (8.0 * DEPTH) ** -0.25
    s_in = D ** -0.5

    def nrm(k, shape, scale):
        return jax.random.normal(k, shape, f32) * scale

    v_scale = jnp.concatenate([jnp.ones((2 * D,), f32), jnp.full((D,), beta, f32)])
    return {
        'x': nrm(ks[0], (BATCH, SEQ, D), 1.0),
        'w_qkv_diff': nrm(ks[1], (n_a, D, 3 * D), s_in) * v_scale,
        'w_o_diff': nrm(ks[2], (n_a, D, D), s_in * beta),
        'lambda_q1': nrm(ks[3], (n_a, HEAD_DIM), 0.1),
        'lambda_k1': nrm(ks[4], (n_a, HEAD_DIM), 0.1),
        'lambda_q2': nrm(ks[5], (n_a, HEAD_DIM), 0.1),
        'lambda_k2': nrm(ks[6], (n_a, HEAD_DIM), 0.1),
        'subln_g': 1.0 + nrm(ks[7], (n_a, 2 * HEAD_DIM), 0.02),
        'w_pool': nrm(ks[8], (n_b, POOL_GROUPS, POOL_WIDTH, POOL_WIDTH), POOL_WIDTH ** -0.5 * beta),
        'pool_scale': 1.0 + nrm(ks[9], (n_b, D), 0.1),
        'w_qkv_sb': nrm(ks[10], (n_c, D, 3 * D), s_in) * v_scale,
        'w_o_sb': nrm(ks[11], (n_c, D, D), s_in * beta),
        'ln_mix_g': 1.0 + nrm(ks[12], (DEPTH, D), 0.02),
        'ln_mix_b': nrm(ks[13], (DEPTH, D), 0.02),
        'ln_ffn_g': 1.0 + nrm(ks[14], (DEPTH, D), 0.02),
        'ln_ffn_b': nrm(ks[15], (DEPTH, D), 0.02),
        'router_w': nrm(ks[16], (D, N_EXPERTS), s_in),
        'router_bias': nrm(ks[17], (N_EXPERTS,), 0.01),
        'w_gate': nrm(ks[18], (DEPTH, N_EXPERTS, D, D_FF_EXPERT), s_in),
        'w_up': nrm(ks[19], (DEPTH, N_EXPERTS, D, D_FF_EXPERT), s_in),
        'w_down': nrm(ks[20], (DEPTH, N_EXPERTS, D_FF_EXPERT, D), D_FF_EXPERT ** -0.5 * beta),
    }


def reference(x, w_qkv_diff, w_o_diff, lambda_q1, lambda_k1, lambda_q2, lambda_k2, subln_g,
              w_pool, pool_scale, w_qkv_sb, w_o_sb, ln_mix_g, ln_mix_b, ln_ffn_g, ln_ffn_b,
              router_w, router_bias, w_gate, w_up, w_down):
    alpha = (2.0 * DEPTH) ** 0.25
    h = x
    for i in range(DEPTH):
        kind, slot = i % N_MIXERS, i // N_MIXERS
        if kind == 0:
            lambda_init = 0.8 - 0.6 * math.exp(-0.3 * i)
            mix = diff_attention(h, w_qkv_diff[slot], w_o_diff[slot], lambda_q1[slot], lambda_k1[slot],
                                 lambda_q2[slot], lambda_k2[slot], subln_g[slot], lambda_init)
        elif kind == 1:
            mix = pool_mixer(h, w_pool[slot], pool_scale[slot])
        else:
            mix = stick_breaking_attention(h, w_qkv_sb[slot], w_o_sb[slot])
        h = layer_norm(alpha * h + mix, ln_mix_g[i], ln_mix_b[i])
        ffn = grouped_moe(h, router_w, router_bias, w_gate[i], w_up[i], w_down[i])
        h = layer_norm(alpha * h + ffn, ln_ffn_g[i], ln_ffn_b[i])
    return h
```

```python
import functools
import math

import jax
import jax.numpy as jnp
from jax import lax
from jax.experimental import pallas as pl
from jax.experimental.pallas import tpu as pltpu

F32 = jnp.float32
BF16 = jnp.bfloat16

HEAD_DIM = 128
CHUNK = 64
N_MIXERS = 3
POOL_WINDOWS = (2, 4, 8, 16)
POOL_HALO = 16
N_GROUPS = 4
ROPE_THETA = 10000.0
LN_EPS = 1e-5
RMS_EPS = 1e-5

LANES = 128
VMEM_LIMIT = 60 * 1024 * 1024
DMA_WINDOW = 32


def _tile(n, pref):
    return pref if n % pref == 0 else n


def _params(n_axes, vmem=VMEM_LIMIT):
    return pltpu.CompilerParams(dimension_semantics=("arbitrary",) * n_axes, vmem_limit_bytes=vmem)


def _proj_kernel(x_ref, w_ref, o_ref, wb_ref):
    @pl.when(pl.program_id(1) == 0)
    def _cast():
        wb_ref[...] = w_ref[...].astype(BF16)

    o_ref[...] = jnp.dot(x_ref[...], wb_ref[...], preferred_element_type=F32).astype(o_ref.dtype)


def _project(x, w, out_dtype):
    m, k = x.shape
    n = w.shape[1]
    tm, tn = _tile(m, 1024), _tile(n, 512)
    return pl.pallas_call(
        _proj_kernel,
        out_shape=jax.ShapeDtypeStruct((m, n), out_dtype),
        grid=(n // tn, m // tm),
        in_specs=[pl.BlockSpec((tm, k), lambda j, i: (i, 0)),
                  pl.BlockSpec((k, tn), lambda j, i: (0, j))],
        out_specs=pl.BlockSpec((tm, tn), lambda j, i: (i, j)),
        scratch_shapes=[pltpu.VMEM((k, tn), BF16)],
        compiler_params=_params(2),
        name="proj",
    )(x, w)


def _qkv_kernel(x_ref, w_ref, cos_ref, sin_ref, o_ref, wb_ref, *, d_model, tn, rope, scale):
    col0 = pl.program_id(0) * tn

    @pl.when(pl.program_id(1) == 0)
    def _cast():
        wb_ref[...] = w_ref[...].astype(BF16)

    acc = jnp.dot(x_ref[...], wb_ref[...], preferred_element_type=F32)
    q_scale = jnp.where(col0 < d_model, scale, 1.0).astype(F32)
    if rope:
        @pl.when(col0 < 2 * d_model)
        def _rotary():
            cos, sin = cos_ref[...], sin_ref[...]
            for c in range(tn // HEAD_DIM):
                t = acc[:, c * HEAD_DIM:(c + 1) * HEAD_DIM]
                r = (t * cos + pltpu.roll(t, HEAD_DIM // 2, axis=1) * sin) * q_scale
                o_ref[:, c * HEAD_DIM:(c + 1) * HEAD_DIM] = r.astype(o_ref.dtype)

        @pl.when(col0 >= 2 * d_model)
        def _plain():
            o_ref[...] = acc.astype(o_ref.dtype)
    else:
        o_ref[...] = (acc * q_scale).astype(o_ref.dtype)


def _qkv_project(x, w, cos, sin, *, seq, rope):
    m, k = x.shape
    n = w.shape[1]
    tm, tn = _tile(seq, 1024), _tile(k, 512)
    n_pos = seq // tm
    kern = functools.partial(_qkv_kernel, d_model=k, tn=tn, rope=rope, scale=HEAD_DIM ** -0.5)
    return pl.pallas_call(
        kern,
        out_shape=jax.ShapeDtypeStruct((m, n), BF16),
        grid=(n // tn, m // tm),
        in_specs=[pl.BlockSpec((tm, k), lambda j, i: (i, 0)),
                  pl.BlockSpec((k, tn), lambda j, i: (0, j)),
                  pl.BlockSpec((tm, HEAD_DIM), lambda j, i: (i % n_pos, 0)),
                  pl.BlockSpec((tm, HEAD_DIM), lambda j, i: (i % n_pos, 0))],
        out_specs=pl.BlockSpec((tm, tn), lambda j, i: (i, j)),
        scratch_shapes=[pltpu.VMEM((k, tn), BF16)],
        compiler_params=_params(2),
        name="qkv_proj",
    )(x, w, cos, sin)


def _rope_tables(seq):
    half = HEAD_DIM // 2
    inv_freq = ROPE_THETA ** (-jnp.arange(half, dtype=F32) / half)
    ang = jnp.arange(seq).astype(F32)[:, None] * inv_freq[None, :]
    cos, sin = jnp.cos(ang), jnp.sin(ang)
    return jnp.concatenate([cos, cos], axis=-1), jnp.concatenate([-sin, sin], axis=-1)


def _nt_dot(a, b):
    return lax.dot_general(a, b, (((1,), (1,)), ((), ())), preferred_element_type=F32)


def _diff_attn_kernel(q1_ref, q2_ref, k1_ref, k2_ref, v_ref, lq1_ref, lk1_ref, lq2_ref, lk2_ref,
                      g_ref, o_ref, m_sc, l_sc, acc_sc, *, t, lambda_init):
    qi = pl.program_id(2)
    m_sc[...] = jnp.full(m_sc.shape, -jnp.inf, F32)
    l_sc[...] = jnp.zeros(l_sc.shape, F32)
    acc_sc[...] = jnp.zeros(acc_sc.shape, F32)
    row_chunk = lax.broadcasted_iota(jnp.int32, (t, t), 0) // CHUNK
    col_chunk = lax.broadcasted_iota(jnp.int32, (t, t), 1) // CHUNK
    visible = col_chunk <= row_chunk

    def step(kb, diagonal):
        k0 = pl.multiple_of(kb * t, t)
        vb = v_ref[pl.ds(k0, t), :]
        for a, (q_ref, k_ref) in enumerate(((q1_ref, k1_ref), (q2_ref, k2_ref))):
            s = _nt_dot(q_ref[...], k_ref[pl.ds(k0, t), :])
            if diagonal:
                s = jnp.where(visible, s, -jnp.inf)
            m_prev = m_sc[a]
            m_new = jnp.maximum(m_prev, jnp.max(s, axis=-1, keepdims=True))
            alpha = jnp.exp(m_prev - m_new)
            p = jnp.exp(s - m_new)
            l_sc[a] = alpha * l_sc[a] + jnp.sum(p, axis=-1, keepdims=True)
            acc_sc[a] = alpha * acc_sc[a] + jnp.dot(p.astype(BF16), vb, preferred_element_type=F32)
            m_sc[a] = m_new

    def body(kb, carry):
        step(kb, False)
        return carry

    lax.fori_loop(0, qi, body, 0)
    step(qi, True)

    lam = (jnp.exp(jnp.sum(lq1_ref[...] * lk1_ref[...], axis=-1, keepdims=True))
           - jnp.exp(jnp.sum(lq2_ref[...] * lk2_ref[...], axis=-1, keepdims=True)) + lambda_init)
    o = acc_sc[0] / l_sc[0] - lam * (acc_sc[1] / l_sc[1])
    o = o * lax.rsqrt(jnp.mean(o * o, axis=-1, keepdims=True) + RMS_EPS) * g_ref[...]
    o_ref[...] = (o * (1.0 - lambda_init)).astype(o_ref.dtype)


def _diff_attention(qkv, lq1, lk1, lq2, lk2, subln_g, *, batch, seq, d_model, lambda_init):
    n_heads = d_model // (2 * HEAD_DIM)
    t = _tile(seq, 256)
    nq = seq // t
    kq = d_model // HEAD_DIM
    kv = 2 * d_model // (2 * HEAD_DIM)
    vec = lambda a: a.reshape(1, -1).astype(F32)
    small = lambda n: pl.BlockSpec((1, n), lambda b, h, i: (0, 0))
    kern = functools.partial(_diff_attn_kernel, t=t, lambda_init=lambda_init)
    return pl.pallas_call(
        kern,
        out_shape=jax.ShapeDtypeStruct((batch * seq, d_model), BF16),
        grid=(batch, n_heads, nq),
        in_specs=[pl.BlockSpec((t, HEAD_DIM), lambda b, h, i: (b * nq + i, 2 * h)),
                  pl.BlockSpec((t, HEAD_DIM), lambda b, h, i: (b * nq + i, 2 * h + 1)),
                  pl.BlockSpec((seq, HEAD_DIM), lambda b, h, i: (b, kq + 2 * h)),
                  pl.BlockSpec((seq, HEAD_DIM), lambda b, h, i: (b, kq + 2 * h + 1)),
                  pl.BlockSpec((seq, 2 * HEAD_DIM), lambda b, h, i: (b, kv + h)),
                  small(HEAD_DIM), small(HEAD_DIM), small(HEAD_DIM), small(HEAD_DIM),
                  small(2 * HEAD_DIM)],
        out_specs=pl.BlockSpec((t, 2 * HEAD_DIM), lambda b, h, i: (b * nq + i, h)),
        scratch_shapes=[pltpu.VMEM((2, t, 1), F32), pltpu.VMEM((2, t, 1), F32),
                        pltpu.VMEM((2, t, 2 * HEAD_DIM), F32)],
        compiler_params=_params(3),
        name="diff_attn",
    )(qkv, qkv, qkv, qkv, qkv, vec(lq1), vec(lk1), vec(lq2), vec(lk2), vec(subln_g))


def _sb_attn_kernel(q_ref, k_ref, v_ref, o_ref, run_sc, acc_sc, *, t):
    qi = pl.program_id(2)
    run_sc[...] = jnp.zeros(run_sc.shape, F32)
    acc_sc[...] = jnp.zeros(acc_sc.shape, F32)
    row = lax.broadcasted_iota(jnp.int32, (t, t), 0)
    col = lax.broadcasted_iota(jnp.int32, (t, t), 1)
    strict = col < row
    after = jnp.where(row > col, 1.0, 0.0).astype(BF16)

    def step(kb, diagonal):
        k0 = pl.multiple_of(kb * t, t)
        z = _nt_dot(q_ref[...], k_ref[pl.ds(k0, t), :])
        softplus = jnp.maximum(z, 0.0) + jnp.log1p(jnp.exp(-jnp.abs(z)))
        log_keep = -softplus
        if diagonal:
            log_keep = jnp.where(strict, log_keep, 0.0)
        hi = log_keep.astype(BF16)
        lo = (log_keep - hi.astype(F32)).astype(BF16)
        later = (jnp.dot(hi, after, preferred_element_type=F32)
                 + jnp.dot(lo, after, preferred_element_type=F32) + run_sc[...])
        w = jnp.exp((z - softplus) + later)
        if diagonal:
            w = jnp.where(strict, w, 0.0)
        acc_sc[...] += jnp.dot(w.astype(BF16), v_ref[pl.ds(k0, t), :], preferred_element_type=F32)
        run_sc[...] += jnp.sum(log_keep, axis=-1, keepdims=True)

    step(qi, True)

    def body(n, carry):
        step(qi - 1 - n, False)
        return carry

    lax.fori_loop(0, qi, body, 0)
    o_ref[...] = acc_sc[...].astype(o_ref.dtype)


def _sb_attention(qkv, *, batch, seq, d_model):
    n_heads = d_model // HEAD_DIM
    t = _tile(seq, 256)
    nq = seq // t
    return pl.pallas_call(
        functools.partial(_sb_attn_kernel, t=t),
        out_shape=jax.ShapeDtypeStruct((batch * seq, d_model), BF16),
        grid=(batch, n_heads, nq),
        in_specs=[pl.BlockSpec((t, HEAD_DIM), lambda b, h, i: (b * nq + i, h)),
                  pl.BlockSpec((seq, HEAD_DIM), lambda b, h, i: (b, n_heads + h)),
                  pl.BlockSpec((seq, HEAD_DIM), lambda b, h, i: (b, 2 * n_heads + h))],
        out_specs=pl.BlockSpec((t, HEAD_DIM), lambda b, h, i: (b * nq + i, h)),
        scratch_shapes=[pltpu.VMEM((t, 1), F32), pltpu.VMEM((t, HEAD_DIM), F32)],
        compiler_params=_params(3),
        name="sb_attn",
    )(qkv, qkv, qkv)


def _pool_kernel(x_ref, halo_ref, w_ref, scale_ref, o_ref, wb_ref, *, tm, seq):
    g = pl.program_id(0)
    i = pl.program_id(1)

    @pl.when(i == 0)
    def _cast():
        wb_ref[...] = w_ref[0].astype(BF16)

    pos0 = (i * tm) % seq
    x = x_ref[...]
    halo = halo_ref[...] * jnp.where(pos0 == 0, 0.0, 1.0).astype(F32)
    xe = jnp.concatenate([halo, x], axis=0)
    pos = pos0 + lax.broadcasted_iota(jnp.int32, (tm, 1), 0)

    for gi, window in enumerate(POOL_WINDOWS):
        @pl.when(g == gi)
        def _window(window=window):
            cur = xe
            span = 1
            while span < window:
                cur = cur + pltpu.roll(cur, span, axis=0)
                span *= 2
            count = jnp.minimum(pos + 1, window).astype(F32)
            pooled = cur[POOL_HALO:, :] / count - x
            y = jnp.dot(pooled.astype(BF16), wb_ref[...], preferred_element_type=F32)
            o_ref[...] = y * scale_ref[...]


def _pool_mixer(h, w_pool, pool_scale, *, seq):
    tokens, d_model = h.shape
    groups, width = w_pool.shape[0], w_pool.shape[1]
    tm = _tile(seq, 512)
    halo_blocks = tm // POOL_HALO
    return pl.pallas_call(
        functools.partial(_pool_kernel, tm=tm, seq=seq),
        out_shape=jax.ShapeDtypeStruct((tokens, d_model), F32),
        grid=(groups, tokens // tm),
        in_specs=[pl.BlockSpec((tm, width), lambda g, i: (i, g)),
                  pl.BlockSpec((POOL_HALO, width), lambda g, i: (jnp.maximum(i * halo_blocks - 1, 0), g)),
                  pl.BlockSpec((1, width, width), lambda g, i: (g, 0, 0)),
                  pl.BlockSpec((1, width), lambda g, i: (0, g))],
        out_specs=pl.BlockSpec((tm, width), lambda g, i: (i, g)),
        scratch_shapes=[pltpu.VMEM((width, width), BF16)],
        compiler_params=_params(2),
        name="pool_mixer",
    )(h, h, w_pool, pool_scale.reshape(1, d_model))


def _pack_bf16_pair(lo, hi):
    lo_bits = pltpu.bitcast(lo.astype(BF16).astype(F32), jnp.uint32) >> 16
    hi_bits = pltpu.bitcast(hi.astype(BF16).astype(F32), jnp.uint32) & jnp.uint32(0xFFFF0000)
    return lo_bits | hi_bits


def _unpack_bf16_pair(packed):
    lo = pltpu.bitcast(packed << 16, F32).astype(BF16)
    hi = pltpu.bitcast(packed & jnp.uint32(0xFFFF0000), F32).astype(BF16)
    return lo, hi


def _layer_norm_rows(y, g, b):
    mu = jnp.mean(y, axis=-1, keepdims=True)
    yc = y - mu
    var = jnp.mean(yc * yc, axis=-1, keepdims=True)
    return yc * lax.rsqrt(var + LN_EPS) * g + b


def _emit_norm(out, o_refs, emit):
    o_refs[0][...] = out
    if emit == "bf16":
        o_refs[1][...] = out.astype(BF16)
    elif emit == "packed":
        half = out.shape[1] // 2
        o_refs[1][...] = _pack_bf16_pair(out[:, :half], out[:, half:])


def _ln_mix_kernel(h_ref, a_ref, g_ref, b_ref, *o_refs, alpha, emit):
    y = alpha * h_ref[...] + a_ref[...].astype(F32)
    _emit_norm(_layer_norm_rows(y, g_ref[...], b_ref[...]), o_refs, emit)


def _ln_moe_kernel(h_ref, y_ref, gate_ref, g_ref, b_ref, *o_refs, alpha, emit):
    gates = gate_ref[...]
    ffn = gates[:, 0:1] * y_ref[0] + gates[:, 1:2] * y_ref[1]
    y = alpha * h_ref[...] + ffn
    _emit_norm(_layer_norm_rows(y, g_ref[...], b_ref[...]), o_refs, emit)


def _norm_outputs(tokens, d_model, tm, emit):
    shapes = [jax.ShapeDtypeStruct((tokens, d_model), F32)]
    specs = [pl.BlockSpec((tm, d_model), lambda i: (i, 0))]
    if emit == "bf16":
        shapes.append(jax.ShapeDtypeStruct((tokens, d_model), BF16))
        specs.append(pl.BlockSpec((tm, d_model), lambda i: (i, 0)))
    elif emit == "packed":
        shapes.append(jax.ShapeDtypeStruct((tokens, d_model // 2), jnp.uint32))
        specs.append(pl.BlockSpec((tm, d_model // 2), lambda i: (i, 0)))
    return shapes, specs


def _ln_mix(h, mix, gain, bias, *, alpha, emit):
    tokens, d_model = h.shape
    tm = _tile(tokens, 256)
    shapes, specs = _norm_outputs(tokens, d_model, tm, emit)
    row = pl.BlockSpec((tm, d_model), lambda i: (i, 0))
    vec = pl.BlockSpec((1, d_model), lambda i: (0, 0))
    return pl.pallas_call(
        functools.partial(_ln_mix_kernel, alpha=alpha, emit=emit),
        out_shape=shapes, grid=(tokens // tm,),
        in_specs=[row, row, vec, vec], out_specs=specs,
        compiler_params=_params(1), name="ln_mix",
    )(h, mix, gain.reshape(1, d_model), bias.reshape(1, d_model))


def _ln_moe(h, y2, gates, gain, bias, *, alpha, emit):
    tokens, d_model = h.shape
    tm = _tile(tokens, 256)
    shapes, specs = _norm_outputs(tokens, d_model, tm, emit)
    row = pl.BlockSpec((tm, d_model), lambda i: (i, 0))
    vec = pl.BlockSpec((1, d_model), lambda i: (0, 0))
    return pl.pallas_call(
        functools.partial(_ln_moe_kernel, alpha=alpha, emit=emit),
        out_shape=shapes, grid=(tokens // tm,),
        in_specs=[row, pl.BlockSpec((2, tm, d_model), lambda i: (0, i, 0)),
                  pl.BlockSpec((tm, 2), lambda i: (i, 0)), vec, vec],
        out_specs=specs,
        compiler_params=_params(1), name="ln_moe",
    )(h, y2, gates, gain.reshape(1, d_model), bias.reshape(1, d_model))


def _first_argmax(vals):
    best = vals[0]
    for v in vals[1:]:
        best = jnp.maximum(best, v)
    idx = jnp.full(best.shape, len(vals) - 1, jnp.int32)
    for j in range(len(vals) - 2, -1, -1):
        idx = jnp.where(vals[j] == best, j, idx)
    return best, idx


def _router_kernel(h_ref, rw_ref, bias_ref, idx_ref, gate_ref, rank_ref, count_ref, run_sc,
                   *, tm, n_experts):
    per_group = n_experts // N_GROUPS

    @pl.when(pl.program_id(0) == 0)
    def _init():
        run_sc[...] = jnp.zeros(run_sc.shape, F32)

    logits = lax.dot_general(rw_ref[...], h_ref[...], (((1,), (1,)), ((), ())),
                             precision=lax.Precision.HIGHEST, preferred_element_type=F32)
    aff = jax.nn.sigmoid(logits)
    sel = aff + bias_ref[...]
    rows = [sel[e:e + 1, :] for e in range(n_experts)]
    aff_rows = [aff[e:e + 1, :] for e in range(n_experts)]

    scores = []
    for g in range(N_GROUPS):
        a, b, c, d = rows[g * per_group:(g + 1) * per_group]
        lo1, hi1 = jnp.minimum(a, b), jnp.maximum(a, b)
        lo2, hi2 = jnp.minimum(c, d), jnp.maximum(c, d)
        scores.append(jnp.maximum(hi1, hi2) + jnp.maximum(jnp.minimum(hi1, hi2), jnp.maximum(lo1, lo2)))
    _, grp = _first_argmax(scores)

    def in_group(table, j):
        out = table[(N_GROUPS - 1) * per_group + j]
        for g in range(N_GROUPS - 2, -1, -1):
            out = jnp.where(grp == g, table[g * per_group + j], out)
        return out

    cand = [in_group(rows, j) for j in range(per_group)]
    cand_aff = [in_group(aff_rows, j) for j in range(per_group)]
    _, loc0 = _first_argmax(cand)
    rest = [jnp.where(loc0 == j, -jnp.inf, cand[j]) for j in range(per_group)]
    _, loc1 = _first_argmax(rest)

    def pick(table, loc):
        out = table[per_group - 1]
        for j in range(per_group - 2, -1, -1):
            out = jnp.where(loc == j, table[j], out)
        return out

    g0, g1 = pick(cand_aff, loc0), pick(cand_aff, loc1)
    total = g0 + g1
    e0 = grp * per_group + loc0
    e1 = grp * per_group + loc1
    idx_ref[0:1, :] = e0
    idx_ref[1:2, :] = e1
    gate_ref[0:1, :] = g0 / total
    gate_ref[1:2, :] = g1 / total

    expert = lax.broadcasted_iota(jnp.int32, (n_experts, tm), 0)
    hit0 = expert == e0
    hit1 = expert == e1
    chosen = jnp.where(hit0, 1.0, jnp.where(hit1, 1.0, 0.0))
    earlier = jnp.where(lax.broadcasted_iota(jnp.int32, (tm, tm), 0)
                        < lax.broadcasted_iota(jnp.int32, (tm, tm), 1), 1.0, 0.0).astype(BF16)
    before = jnp.dot(chosen.astype(BF16), earlier, preferred_element_type=F32) + run_sc[...]
    rank_ref[0:1, :] = jnp.sum(jnp.where(hit0, before, 0.0), axis=0, keepdims=True).astype(jnp.int32)
    rank_ref[1:2, :] = jnp.sum(jnp.where(hit1, before, 0.0), axis=0, keepdims=True).astype(jnp.int32)
    run = run_sc[...] + jnp.sum(chosen, axis=1, keepdims=True)
    run_sc[...] = run
    count_ref[...] = jnp.broadcast_to(run, count_ref.shape).astype(jnp.int32)


def _route(h, router_w, router_bias):
    tokens, d_model = h.shape
    n_experts = router_w.shape[1]
    tm = _tile(tokens, 512)
    out2 = lambda dt: jax.ShapeDtypeStruct((2, tokens), dt)
    spec2 = pl.BlockSpec((2, tm), lambda i: (0, i))
    return pl.pallas_call(
        functools.partial(_router_kernel, tm=tm, n_experts=n_experts),
        out_shape=[out2(jnp.int32), out2(F32), out2(jnp.int32),
                   jax.ShapeDtypeStruct((n_experts, LANES), jnp.int32)],
        grid=(tokens // tm,),
        in_specs=[pl.BlockSpec((tm, d_model), lambda i: (i, 0)),
                  pl.BlockSpec((n_experts, d_model), lambda i: (0, 0)),
                  pl.BlockSpec((n_experts, 1), lambda i: (0, 0))],
        out_specs=[spec2, spec2, spec2, pl.BlockSpec((n_experts, LANES), lambda i: (0, 0))],
        scratch_shapes=[pltpu.VMEM((n_experts, 1), F32)],
        compiler_params=_params(1),
        name="router",
    )(h, router_w.T, router_bias.reshape(n_experts, 1).astype(F32))


def _row_move_kernel(src_idx_ref, dst_idx_ref, zero_row_ref, zero_on_ref, src_ref, dst_ref,
                     zero_sc, sems, *, n_rows, n_zero, window):
    if n_zero:
        zero_sc[...] = jnp.zeros(zero_sc.shape, zero_sc.dtype)
        rows = zero_sc.shape[0]

        def zero_copy(e):
            start = pl.multiple_of(zero_row_ref[e], rows)
            return pltpu.make_async_copy(zero_sc, dst_ref.at[pl.ds(start, rows)], sems.at[0])

        for e in range(n_zero):
            @pl.when(zero_on_ref[e] > 0)
            def _start(e=e):
                zero_copy(e).start()
        for e in range(n_zero):
            @pl.when(zero_on_ref[e] > 0)
            def _wait(e=e):
                zero_copy(e).wait()

    def row_copy(i):
        return pltpu.make_async_copy(src_ref.at[pl.ds(src_idx_ref[i], 1)],
                                     dst_ref.at[pl.ds(dst_idx_ref[i], 1)], sems.at[1])

    def prime(i, carry):
        row_copy(i).start()
        return carry

    def steady(i, carry):
        row_copy(i - window).wait()
        row_copy(i).start()
        return carry

    def drain(i, carry):
        row_copy(i).wait()
        return carry

    lax.fori_loop(0, window, prime, 0)
    lax.fori_loop(window, n_rows, steady, 0)
    lax.fori_loop(n_rows - window, n_rows, drain, 0)


def _move_rows(src, src_idx, dst_idx, n_dst, zero_rows=None, zero_on=None, zero_block=8):
    n_rows = src_idx.shape[0]
    width = src.shape[1]
    n_zero = 0 if zero_rows is None else zero_rows.shape[0]
    if zero_rows is None:
        zero_rows = jnp.zeros((1,), jnp.int32)
        zero_on = jnp.zeros((1,), jnp.int32)
    window = min(DMA_WINDOW, n_rows)
    kern = functools.partial(_row_move_kernel, n_rows=n_rows, n_zero=n_zero, window=window)
    return pl.pallas_call(
        kern,
        out_shape=jax.ShapeDtypeStruct((n_dst, width), src.dtype),
        grid_spec=pltpu.PrefetchScalarGridSpec(
            num_scalar_prefetch=4, grid=(1,),
            in_specs=[pl.BlockSpec(memory_space=pl.ANY)],
            out_specs=pl.BlockSpec(memory_space=pl.ANY),
            scratch_shapes=[pltpu.VMEM((zero_block, width), src.dtype),
                            pltpu.SemaphoreType.DMA((2,))]),
        compiler_params=_params(1),
        name="move_rows",
    )(src_idx, dst_idx, zero_rows, zero_on, src)


def _cast_kernel(x_ref, o_ref):
    o_ref[...] = x_ref[...].astype(o_ref.dtype)


def _to_bf16(w):
    n, r, c = w.shape
    tr = _tile(r, 1024)
    return pl.pallas_call(
        _cast_kernel,
        out_shape=jax.ShapeDtypeStruct(w.shape, BF16),
        grid=(n, r // tr),
        in_specs=[pl.BlockSpec((1, tr, c), lambda a, b: (a, b, 0))],
        out_specs=pl.BlockSpec((1, tr, c), lambda a, b: (a, b, 0)),
        compiler_params=_params(2),
        name="to_bf16",
    )(w)


def _expert_kernel(tile_expert_ref, tile_block_ref, n_used_ref, x_ref, wg_ref, wu_ref, wd_ref, o_ref,
                   *, d_model):
    del tile_expert_ref, tile_block_ref
    half = d_model // 2

    @pl.when(pl.program_id(0) >= n_used_ref[0])
    def _unused_tile():
        o_ref[...] = jnp.zeros(o_ref.shape, o_ref.dtype)

    @pl.when(pl.program_id(0) < n_used_ref[0])
    def _compute():
        lo, hi = _unpack_bf16_pair(x_ref[...])
        gate = (jnp.dot(lo, wg_ref[0, :half, :], preferred_element_type=F32)
                + jnp.dot(hi, wg_ref[0, half:, :], preferred_element_type=F32))
        up = (jnp.dot(lo, wu_ref[0, :half, :], preferred_element_type=F32)
              + jnp.dot(hi, wu_ref[0, half:, :], preferred_element_type=F32))
        act = (gate * jax.nn.sigmoid(gate) * up).astype(BF16)
        col = _tile(d_model, 1024)
        for c in range(d_model // col):
            o_ref[:, c * col:(c + 1) * col] = jnp.dot(act, wd_ref[0, :, c * col:(c + 1) * col],
                                                      preferred_element_type=F32)


def _expert_mlp(xs, wg, wu, wd, tile_expert, tile_block, n_used, *, layer, n_experts, tm):
    n_slots, half = xs.shape
    d_model = 2 * half
    d_ff = wg.shape[2]
    n_tiles = n_slots // tm
    base = layer * n_experts
    return pl.pallas_call(
        functools.partial(_expert_kernel, d_model=d_model),
        out_shape=jax.ShapeDtypeStruct((n_slots, d_model), F32),
        grid_spec=pltpu.PrefetchScalarGridSpec(
            num_scalar_prefetch=3, grid=(n_tiles,),
            in_specs=[pl.BlockSpec((tm, half), lambda i, te, tb, nu: (tb[i], 0)),
                      pl.BlockSpec((1, d_model, d_ff), lambda i, te, tb, nu: (base + te[i], 0, 0)),
                      pl.BlockSpec((1, d_model, d_ff), lambda i, te, tb, nu: (base + te[i], 0, 0)),
                      pl.BlockSpec((1, d_ff, d_model), lambda i, te, tb, nu: (base + te[i], 0, 0))],
            out_specs=pl.BlockSpec((tm, d_model), lambda i, te, tb, nu: (i, 0))),
        compiler_params=_params(1),
        name="expert_mlp",
    )(tile_expert, tile_block, n_used, xs, wg, wu, wd)


def _grouped_moe(h, hp, router_w, router_bias, wg, wu, wd, *, layer, tm):
    tokens, d_model = h.shape
    n_experts = router_w.shape[1]
    idx, gates, rank, counts = _route(h, router_w, router_bias)

    counts = counts[:, 0]
    padded = ((counts + tm - 1) // tm) * tm
    seg_end = jnp.cumsum(padded)
    seg_start = seg_end - padded
    n_tiles = (2 * tokens) // tm + n_experts
    n_slots = n_tiles * tm
    slot = (jnp.take(seg_start, idx.reshape(-1)) + rank.reshape(-1)).astype(jnp.int32)
    n_used = (seg_end[-1] // tm).astype(jnp.int32)
    tile_block = jnp.minimum(jnp.arange(n_tiles, dtype=jnp.int32), n_used - 1)
    tile_expert = jnp.sum(seg_end[None, :] <= (tile_block * tm)[:, None], axis=1).astype(jnp.int32)
    tile_expert = jnp.minimum(tile_expert, n_experts - 1)
    spare = n_used + jnp.arange(n_experts, dtype=jnp.int32)
    zero_rows = jnp.concatenate([jnp.maximum(seg_end - tm, 0).astype(jnp.int32),
                                 jnp.minimum(spare, n_tiles - 1) * tm])
    zero_on = jnp.concatenate([padded > counts, spare < n_tiles]).astype(jnp.int32)

    token_of = jnp.tile(jnp.arange(tokens, dtype=jnp.int32), 2)
    xs = _move_rows(hp, token_of, slot, n_slots, zero_rows, zero_on, zero_block=tm)
    ys = _expert_mlp(xs, wg, wu, wd, tile_expert, tile_block, n_used.reshape(1),
                     layer=layer, n_experts=n_experts, tm=tm)
    y2 = _move_rows(ys, slot, jnp.arange(2 * tokens, dtype=jnp.int32), 2 * tokens)
    return y2.reshape(2, tokens, d_model), gates.T


def kernel(x, w_qkv_diff, w_o_diff, lambda_q1, lambda_k1, lambda_q2, lambda_k2, subln_g, w_pool,
           pool_scale, w_qkv_sb, w_o_sb, ln_mix_g, ln_mix_b, ln_ffn_g, ln_ffn_b, router_w,
           router_bias, w_gate, w_up, w_down):
    batch, seq, d_model = x.shape
    depth = ln_mix_g.shape[0]
    n_experts = router_w.shape[1]
    d_ff = w_gate.shape[-1]
    tokens = batch * seq
    alpha = (2.0 * depth) ** 0.25
    moe_tile = _tile(2 * tokens, 256)

    wg = _to_bf16(w_gate.reshape(depth * n_experts, d_model, d_ff))
    wu = _to_bf16(w_up.reshape(depth * n_experts, d_model, d_ff))
    wd = _to_bf16(w_down.reshape(depth * n_experts, d_ff, d_model))
    cos, sin = _rope_tables(seq)

    h = x.reshape(tokens, d_model)
    hb = None
    for i in range(depth):
        kind, slot = i % N_MIXERS, i // N_MIXERS
        if kind == 1:
            mix = _pool_mixer(h, w_pool[slot], pool_scale[slot], seq=seq)
        else:
            if hb is None:
                hb = h.astype(BF16)
            if kind == 0:
                lambda_init = 0.8 - 0.6 * math.exp(-0.3 * i)
                qkv = _qkv_project(hb, w_qkv_diff[slot], cos, sin, seq=seq, rope=True)
                o = _diff_attention(qkv, lambda_q1[slot], lambda_k1[slot], lambda_q2[slot],
                                    lambda_k2[slot], subln_g[slot], batch=batch, seq=seq,
                                    d_model=d_model, lambda_init=lambda_init)
                mix = _project(o, w_o_diff[slot], F32)
            else:
                qkv = _qkv_project(hb, w_qkv_sb[slot], cos, sin, seq=seq, rope=False)
                o = _sb_attention(qkv, batch=batch, seq=seq, d_model=d_model)
                mix = _project(o, w_o_sb[slot], F32)
        h, hp = _ln_mix(h, mix, ln_mix_g[i], ln_mix_b[i], alpha=alpha, emit="packed")
        y2, gates = _grouped_moe(h, hp, router_w, router_bias, wg, wu, wd, layer=i, tm=moe_tile)
        next_kind = (i + 1) % N_MIXERS
        emit = "bf16" if (i + 1 < depth and next_kind != 1) else "none"
        outs = _ln_moe(h, y2, gates, ln_ffn_g[i], ln_ffn_b[i], alpha=alpha, emit=emit)
        h = outs[0]
        hb = outs[1] if emit == "bf16" else None
    return h.reshape(batch, seq, d_model)
```

```python
import functools
import math

import jax
import jax.numpy as jnp
from jax import lax
from jax.experimental import pallas as pl
from jax.experimental.pallas import tpu as pltpu

F32 = jnp.float32
BF16 = jnp.bfloat16

HEAD_DIM = 128
CHUNK = 64
N_MIXERS = 3
POOL_WINDOWS = (2, 4, 8, 16)
POOL_HALO = 16
N_GROUPS = 4
ROPE_THETA = 10000.0
LN_EPS = 1e-5
RMS_EPS = 1e-5

LANES = 128
VMEM_LIMIT = 60 * 1024 * 1024
DMA_WINDOW = 32
ATTN_ROWS = 512
SB_KEYS = 256


def _tile(n, pref):
    return pref if n % pref == 0 else n


def _params(n_axes, vmem=VMEM_LIMIT):
    return pltpu.CompilerParams(dimension_semantics=("arbitrary",) * n_axes, vmem_limit_bytes=vmem)


def _slab_load(ref, rows, segs, lead=()):
    return jnp.concatenate([ref[lead + (pl.ds(s, rows, stride=segs), slice(None))] for s in range(segs)],
                           axis=1)


def _slab_store(ref, value, rows, segs):
    for s in range(segs):
        ref[pl.ds(s, rows, stride=segs), :] = value[:, s * LANES:(s + 1) * LANES]


def _proj_kernel(x_ref, w_ref, o_ref, wb_ref):
    @pl.when(pl.program_id(1) == 0)
    def _cast():
        wb_ref[...] = w_ref[0].astype(BF16)

    o_ref[...] = jnp.dot(x_ref[...], wb_ref[...], preferred_element_type=F32).astype(o_ref.dtype)


def _project(x, w, slot, out_dtype):
    m, k = x.shape
    n = w.shape[2]
    tm, tn = _tile(m, 1024), _tile(n, 512)
    return pl.pallas_call(
        _proj_kernel,
        out_shape=jax.ShapeDtypeStruct((m, n), out_dtype),
        grid=(n // tn, m // tm),
        in_specs=[pl.BlockSpec((tm, k), lambda j, i: (i, 0)),
                  pl.BlockSpec((1, k, tn), lambda j, i: (slot, 0, j))],
        out_specs=pl.BlockSpec((tm, tn), lambda j, i: (i, j)),
        scratch_shapes=[pltpu.VMEM((k, tn), BF16)],
        compiler_params=_params(2),
        name="proj",
    )(x, w)


def _qkv_kernel(x_ref, w_ref, cos_ref, sin_ref, o_ref, wb_ref, *, d_model, tn, rope, scale):
    col0 = pl.program_id(0) * tn

    @pl.when(pl.program_id(1) == 0)
    def _cast():
        wb_ref[...] = w_ref[0].astype(BF16)

    acc = jnp.dot(x_ref[...], wb_ref[...], preferred_element_type=F32)
    q_scale = jnp.where(col0 < d_model, scale, 1.0).astype(F32)
    if rope:
        @pl.when(col0 < 2 * d_model)
        def _rotary():
            cos, sin = cos_ref[...], sin_ref[...]
            for c in range(tn // HEAD_DIM):
                t = acc[:, c * HEAD_DIM:(c + 1) * HEAD_DIM]
                r = (t * cos + pltpu.roll(t, HEAD_DIM // 2, axis=1) * sin) * q_scale
                o_ref[:, c * HEAD_DIM:(c + 1) * HEAD_DIM] = r.astype(o_ref.dtype)

        @pl.when(col0 >= 2 * d_model)
        def _plain():
            o_ref[...] = acc.astype(o_ref.dtype)
    else:
        o_ref[...] = (acc * q_scale).astype(o_ref.dtype)


def _qkv_project(x, w, slot, cos, sin, *, seq, rope):
    m, k = x.shape
    n = w.shape[2]
    tm, tn = _tile(seq, 1024), _tile(k, 512)
    n_pos = seq // tm
    kern = functools.partial(_qkv_kernel, d_model=k, tn=tn, rope=rope, scale=HEAD_DIM ** -0.5)
    return pl.pallas_call(
        kern,
        out_shape=jax.ShapeDtypeStruct((m, n), BF16),
        grid=(n // tn, m // tm),
        in_specs=[pl.BlockSpec((tm, k), lambda j, i: (i, 0)),
                  pl.BlockSpec((1, k, tn), lambda j, i: (slot, 0, j)),
                  pl.BlockSpec((tm, HEAD_DIM), lambda j, i: (i % n_pos, 0)),
                  pl.BlockSpec((tm, HEAD_DIM), lambda j, i: (i % n_pos, 0))],
        out_specs=pl.BlockSpec((tm, tn), lambda j, i: (i, j)),
        scratch_shapes=[pltpu.VMEM((k, tn), BF16)],
        compiler_params=_params(2),
        name="qkv_proj",
    )(x, w, cos, sin)


def _rope_tables(seq):
    half = HEAD_DIM // 2
    inv_freq = ROPE_THETA ** (-jnp.arange(half, dtype=F32) / half)
    ang = jnp.arange(seq).astype(F32)[:, None] * inv_freq[None, :]
    cos, sin = jnp.cos(ang), jnp.sin(ang)
    return jnp.concatenate([cos, cos], axis=-1), jnp.concatenate([-sin, sin], axis=-1)


def _nt_dot(a, b):
    return lax.dot_general(a, b, (((1,), (1,)), ((), ())), preferred_element_type=F32)


def _diff_attn_kernel(q1_ref, q2_ref, k1_ref, k2_ref, v_ref, lq1_ref, lk1_ref, lq2_ref, lk2_ref,
                      g_ref, o_ref, *, seq, t, lambda_init):
    lam = (jnp.exp(jnp.sum(lq1_ref[...] * lk1_ref[...], axis=-1, keepdims=True))
           - jnp.exp(jnp.sum(lq2_ref[...] * lk2_ref[...], axis=-1, keepdims=True)) + lambda_init)
    row_chunk = lax.broadcasted_iota(jnp.int32, (t, t), 0) // CHUNK
    col_chunk = lax.broadcasted_iota(jnp.int32, (t, t), 1) // CHUNK
    visible = col_chunk <= row_chunk

    for qi in range(seq // t):
        r0 = qi * t
        heads = []
        for q_ref, k_ref in ((q1_ref, k1_ref), (q2_ref, k2_ref)):
            q = q_ref[r0:r0 + t, :]
            s_diag = jnp.where(visible, _nt_dot(q, k_ref[r0:r0 + t, :]), -jnp.inf)
            m = jnp.max(s_diag, axis=-1, keepdims=True)
            if qi:
                s_past = _nt_dot(q, k_ref[0:r0, :])
                m = jnp.maximum(m, jnp.max(s_past, axis=-1, keepdims=True))
            p = jnp.exp(s_diag - m)
            denom = jnp.sum(p, axis=-1, keepdims=True)
            acc = jnp.dot(p.astype(BF16), v_ref[r0:r0 + t, :], preferred_element_type=F32)
            if qi:
                p = jnp.exp(s_past - m)
                denom = denom + jnp.sum(p, axis=-1, keepdims=True)
                acc = acc + jnp.dot(p.astype(BF16), v_ref[0:r0, :], preferred_element_type=F32)
            heads.append(acc / denom)
        o = heads[0] - lam * heads[1]
        o = o * lax.rsqrt(jnp.mean(o * o, axis=-1, keepdims=True) + RMS_EPS) * g_ref[...]
        o_ref[r0:r0 + t, :] = (o * (1.0 - lambda_init)).astype(o_ref.dtype)


def _diff_attention(qkv, lq1, lk1, lq2, lk2, subln_g, *, batch, seq, d_model, lambda_init):
    n_heads = d_model // (2 * HEAD_DIM)
    t = _tile(seq, ATTN_ROWS)
    kq = d_model // HEAD_DIM
    kv = 2 * d_model // (2 * HEAD_DIM)
    vec = lambda a: a.reshape(1, -1).astype(F32)
    small = lambda n: pl.BlockSpec((1, n), lambda b, h: (0, 0))
    kern = functools.partial(_diff_attn_kernel, seq=seq, t=t, lambda_init=lambda_init)
    return pl.pallas_call(
        kern,
        out_shape=jax.ShapeDtypeStruct((batch * seq, d_model), BF16),
        grid=(batch, n_heads),
        in_specs=[pl.BlockSpec((seq, HEAD_DIM), lambda b, h: (b, 2 * h)),
                  pl.BlockSpec((seq, HEAD_DIM), lambda b, h: (b, 2 * h + 1)),
                  pl.BlockSpec((seq, HEAD_DIM), lambda b, h: (b, kq + 2 * h)),
                  pl.BlockSpec((seq, HEAD_DIM), lambda b, h: (b, kq + 2 * h + 1)),
                  pl.BlockSpec((seq, 2 * HEAD_DIM), lambda b, h: (b, kv + h)),
                  small(HEAD_DIM), small(HEAD_DIM), small(HEAD_DIM), small(HEAD_DIM),
                  small(2 * HEAD_DIM)],
        out_specs=pl.BlockSpec((seq, 2 * HEAD_DIM), lambda b, h: (b, h)),
        compiler_params=_params(2),
        name="diff_attn",
    )(qkv, qkv, qkv, qkv, qkv, vec(lq1), vec(lk1), vec(lq2), vec(lk2), vec(subln_g))


def _sb_attn_kernel(q_ref, k_ref, v_ref, o_ref, *, seq, t, kb):
    src = lax.broadcasted_iota(jnp.int32, (kb, kb + LANES), 0)
    dst = lax.broadcasted_iota(jnp.int32, (kb, kb + LANES), 1)
    after = jnp.where((src > dst) | (dst >= kb), 1.0, 0.0).astype(BF16)
    row = lax.broadcasted_iota(jnp.int32, (t, kb), 0)
    col = lax.broadcasted_iota(jnp.int32, (t, kb), 1)

    for qi in range(seq // t):
        r0 = qi * t
        q = q_ref[r0:r0 + t, :]
        run = jnp.zeros((t, LANES), F32)
        acc = jnp.zeros((t, HEAD_DIM), F32)
        for j in range((r0 + t) // kb - 1, -1, -1):
            c0 = j * kb
            z = _nt_dot(q, k_ref[c0:c0 + kb, :])
            log1p_e = jnp.log(1.0 + jnp.exp(-jnp.abs(z)))
            log_keep = -(jnp.maximum(z, 0.0) + log1p_e)
            on_diagonal = c0 + kb > r0
            if on_diagonal:
                strict = col - row < r0 - c0
                log_keep = jnp.where(strict, log_keep, 0.0)
            hi = log_keep.astype(BF16)
            lo = (log_keep - hi.astype(F32)).astype(BF16)
            sums = (jnp.dot(hi, after, preferred_element_type=F32)
                    + jnp.dot(lo, after, preferred_element_type=F32))
            later = sums[:, :kb] + jnp.concatenate([run] * (kb // LANES), axis=1)
            w = jnp.exp((jnp.minimum(z, 0.0) - log1p_e) + later)
            if on_diagonal:
                w = jnp.where(strict, w, 0.0)
            acc = acc + jnp.dot(w.astype(BF16), v_ref[c0:c0 + kb, :], preferred_element_type=F32)
            run = run + sums[:, kb:]
        o_ref[r0:r0 + t, :] = acc.astype(o_ref.dtype)


def _sb_attention(qkv, *, batch, seq, d_model):
    n_heads = d_model // HEAD_DIM
    t = _tile(seq, ATTN_ROWS)
    kb = _tile(t, SB_KEYS)
    blk = lambda first: pl.BlockSpec((seq, HEAD_DIM), lambda b, h: (b, first + h))
    return pl.pallas_call(
        functools.partial(_sb_attn_kernel, seq=seq, t=t, kb=kb),
        out_shape=jax.ShapeDtypeStruct((batch * seq, d_model), BF16),
        grid=(batch, n_heads),
        in_specs=[blk(0), blk(n_heads), blk(2 * n_heads)],
        out_specs=blk(0),
        compiler_params=_params(2),
        name="sb_attn",
    )(qkv, qkv, qkv)


def _pool_kernel(x_ref, halo_ref, w_ref, scale_ref, o_ref, wb_ref, *, tm, seq):
    g = pl.program_id(0)
    i = pl.program_id(1)

    @pl.when(i == 0)
    def _cast():
        wb_ref[...] = w_ref[0, 0].astype(BF16)

    pos0 = (i * tm) % seq
    x = x_ref[...]
    halo = halo_ref[...] * jnp.where(pos0 == 0, 0.0, 1.0).astype(F32)
    xe = jnp.concatenate([halo, x], axis=0)
    pos = pos0 + lax.broadcasted_iota(jnp.int32, (tm, 1), 0)

    for gi, window in enumerate(POOL_WINDOWS):
        @pl.when(g == gi)
        def _window(window=window):
            cur = xe
            span = 1
            while span < window:
                cur = cur + pltpu.roll(cur, span, axis=0)
                span *= 2
            count = jnp.minimum(pos + 1, window).astype(F32)
            pooled = cur[POOL_HALO:, :] / count - x
            y = jnp.dot(pooled.astype(BF16), wb_ref[...], preferred_element_type=F32)
            o_ref[...] = y * scale_ref[0]


def _pool_mixer(h, w_pool, pool_scale, slot, *, seq):
    tokens, d_model = h.shape
    groups, width = w_pool.shape[1], w_pool.shape[2]
    tm = _tile(seq, 512)
    halo_blocks = tm // POOL_HALO
    return pl.pallas_call(
        functools.partial(_pool_kernel, tm=tm, seq=seq),
        out_shape=jax.ShapeDtypeStruct((tokens, d_model), F32),
        grid=(groups, tokens // tm),
        in_specs=[pl.BlockSpec((tm, width), lambda g, i: (i, g)),
                  pl.BlockSpec((POOL_HALO, width), lambda g, i: (jnp.maximum(i * halo_blocks - 1, 0), g)),
                  pl.BlockSpec((1, 1, width, width), lambda g, i: (slot, g, 0, 0)),
                  pl.BlockSpec((1, 1, width), lambda g, i: (slot, 0, g))],
        out_specs=pl.BlockSpec((tm, width), lambda g, i: (i, g)),
        scratch_shapes=[pltpu.VMEM((width, width), BF16)],
        compiler_params=_params(2),
        name="pool_mixer",
    )(h, h, w_pool, pool_scale.reshape(pool_scale.shape[0], 1, d_model))


def _pack_bf16_pair(lo, hi):
    lo_bits = pltpu.bitcast(lo.astype(BF16).astype(F32), jnp.uint32) >> 16
    hi_bits = pltpu.bitcast(hi.astype(BF16).astype(F32), jnp.uint32) & jnp.uint32(0xFFFF0000)
    return lo_bits | hi_bits


def _unpack_bf16_pair(packed):
    lo = pltpu.bitcast(packed << 16, F32).astype(BF16)
    hi = pltpu.bitcast(packed & jnp.uint32(0xFFFF0000), F32).astype(BF16)
    return lo, hi


def _layer_norm_rows(y, g, b):
    mu = jnp.mean(y, axis=-1, keepdims=True)
    yc = y - mu
    var = jnp.mean(yc * yc, axis=-1, keepdims=True)
    return yc * lax.rsqrt(var + LN_EPS) * g + b


def _emit_norm(out, o_refs, emit):
    o_refs[0][...] = out
    if emit == "bf16":
        o_refs[1][...] = out.astype(BF16)
    elif emit == "packed":
        rows, half = out.shape[0], out.shape[1] // 2
        _slab_store(o_refs[1], _pack_bf16_pair(out[:, :half], out[:, half:]), rows, half // LANES)


def _ln_mix_kernel(h_ref, a_ref, g_ref, b_ref, *o_refs, alpha, emit):
    y = alpha * h_ref[...] + a_ref[...].astype(F32)
    _emit_norm(_layer_norm_rows(y, g_ref[0], b_ref[0]), o_refs, emit)


def _ln_moe_kernel(h_ref, y_ref, gate_ref, g_ref, b_ref, *o_refs, alpha, emit):
    rows, d_model = h_ref.shape
    gates = gate_ref[...]
    segs = d_model // LANES
    ffn = (gates[:, 0:1] * _slab_load(y_ref, rows, segs, lead=(0,))
           + gates[:, 1:2] * _slab_load(y_ref, rows, segs, lead=(1,)))
    y = alpha * h_ref[...] + ffn
    _emit_norm(_layer_norm_rows(y, g_ref[0], b_ref[0]), o_refs, emit)


def _norm_outputs(tokens, d_model, tm, emit):
    shapes = [jax.ShapeDtypeStruct((tokens, d_model), F32)]
    specs = [pl.BlockSpec((tm, d_model), lambda i: (i, 0))]
    if emit == "bf16":
        shapes.append(jax.ShapeDtypeStruct((tokens, d_model), BF16))
        specs.append(pl.BlockSpec((tm, d_model), lambda i: (i, 0)))
    elif emit == "packed":
        segs = d_model // 2 // LANES
        shapes.append(jax.ShapeDtypeStruct((tokens * segs, LANES), jnp.uint32))
        specs.append(pl.BlockSpec((tm * segs, LANES), lambda i: (i, 0)))
    return shapes, specs


def _ln_mix(h, mix, gain, bias, layer, *, alpha, emit):
    tokens, d_model = h.shape
    tm = _tile(tokens, 256)
    shapes, specs = _norm_outputs(tokens, d_model, tm, emit)
    row = pl.BlockSpec((tm, d_model), lambda i: (i, 0))
    vec = pl.BlockSpec((1, 1, d_model), lambda i: (layer, 0, 0))
    depth = gain.shape[0]
    return pl.pallas_call(
        functools.partial(_ln_mix_kernel, alpha=alpha, emit=emit),
        out_shape=shapes, grid=(tokens // tm,),
        in_specs=[row, row, vec, vec], out_specs=specs,
        compiler_params=_params(1), name="ln_mix",
    )(h, mix, gain.reshape(depth, 1, d_model), bias.reshape(depth, 1, d_model))


def _ln_moe(h, y2, gates, gain, bias, layer, *, alpha, emit):
    tokens, d_model = h.shape
    tm = _tile(tokens, 256)
    segs = d_model // LANES
    shapes, specs = _norm_outputs(tokens, d_model, tm, emit)
    row = pl.BlockSpec((tm, d_model), lambda i: (i, 0))
    vec = pl.BlockSpec((1, 1, d_model), lambda i: (layer, 0, 0))
    depth = gain.shape[0]
    return pl.pallas_call(
        functools.partial(_ln_moe_kernel, alpha=alpha, emit=emit),
        out_shape=shapes, grid=(tokens // tm,),
        in_specs=[row, pl.BlockSpec((2, tm * segs, LANES), lambda i: (0, i, 0)),
                  pl.BlockSpec((tm, 2), lambda i: (i, 0)), vec, vec],
        out_specs=specs,
        compiler_params=_params(1), name="ln_moe",
    )(h, y2, gates, gain.reshape(depth, 1, d_model), bias.reshape(depth, 1, d_model))


def _first_argmax(vals):
    best = vals[0]
    for v in vals[1:]:
        best = jnp.maximum(best, v)
    idx = jnp.full(best.shape, len(vals) - 1, jnp.int32)
    for j in range(len(vals) - 2, -1, -1):
        idx = jnp.where(vals[j] == best, j, idx)
    return best, idx


def _router_kernel(h_ref, rw_ref, bias_ref, idx_ref, gate_ref, rank_ref, count_ref, run_sc,
                   *, tm, n_experts):
    per_group = n_experts // N_GROUPS

    @pl.when(pl.program_id(0) == 0)
    def _init():
        run_sc[...] = jnp.zeros(run_sc.shape, F32)

    logits = lax.dot_general(rw_ref[...], h_ref[...], (((1,), (1,)), ((), ())),
                             precision=lax.Precision.HIGHEST, preferred_element_type=F32)
    aff = jax.nn.sigmoid(logits)
    sel = aff + bias_ref[...]
    rows = [sel[e:e + 1, :] for e in range(n_experts)]
    aff_rows = [aff[e:e + 1, :] for e in range(n_experts)]

    scores = []
    for g in range(N_GROUPS):
        a, b, c, d = rows[g * per_group:(g + 1) * per_group]
        lo1, hi1 = jnp.minimum(a, b), jnp.maximum(a, b)
        lo2, hi2 = jnp.minimum(c, d), jnp.maximum(c, d)
        scores.append(jnp.maximum(hi1, hi2) + jnp.maximum(jnp.minimum(hi1, hi2), jnp.maximum(lo1, lo2)))
    _, grp = _first_argmax(scores)

    def in_group(table, j):
        out = table[(N_GROUPS - 1) * per_group + j]
        for g in range(N_GROUPS - 2, -1, -1):
            out = jnp.where(grp == g, table[g * per_group + j], out)
        return out

    cand = [in_group(rows, j) for j in range(per_group)]
    cand_aff = [in_group(aff_rows, j) for j in range(per_group)]
    _, loc0 = _first_argmax(cand)
    rest = [jnp.where(loc0 == j, -jnp.inf, cand[j]) for j in range(per_group)]
    _, loc1 = _first_argmax(rest)

    def pick(table, loc):
        out = table[per_group - 1]
        for j in range(per_group - 2, -1, -1):
            out = jnp.where(loc == j, table[j], out)
        return out

    g0, g1 = pick(cand_aff, loc0), pick(cand_aff, loc1)
    total = g0 + g1
    e0 = grp * per_group + loc0
    e1 = grp * per_group + loc1
    idx_ref[0:1, :] = e0
    idx_ref[1:2, :] = e1
    gate_ref[0:1, :] = g0 / total
    gate_ref[1:2, :] = g1 / total

    expert = lax.broadcasted_iota(jnp.int32, (n_experts, tm), 0)
    hit0 = expert == e0
    hit1 = expert == e1
    chosen = jnp.where(hit0, 1.0, jnp.where(hit1, 1.0, 0.0))
    earlier = jnp.where(lax.broadcasted_iota(jnp.int32, (tm, tm), 0)
                        < lax.broadcasted_iota(jnp.int32, (tm, tm), 1), 1.0, 0.0).astype(BF16)
    before = jnp.dot(chosen.astype(BF16), earlier, preferred_element_type=F32) + run_sc[...]
    rank_ref[0:1, :] = jnp.sum(jnp.where(hit0, before, 0.0), axis=0, keepdims=True).astype(jnp.int32)
    rank_ref[1:2, :] = jnp.sum(jnp.where(hit1, before, 0.0), axis=0, keepdims=True).astype(jnp.int32)
    run = run_sc[...] + jnp.sum(chosen, axis=1, keepdims=True)
    run_sc[...] = run
    count_ref[...] = jnp.broadcast_to(run, count_ref.shape).astype(jnp.int32)


def _route(h, router_w, router_bias):
    tokens, d_model = h.shape
    n_experts = router_w.shape[1]
    tm = _tile(tokens, 512)
    out2 = lambda dt: jax.ShapeDtypeStruct((2, tokens), dt)
    spec2 = pl.BlockSpec((2, tm), lambda i: (0, i))
    return pl.pallas_call(
        functools.partial(_router_kernel, tm=tm, n_experts=n_experts),
        out_shape=[out2(jnp.int32), out2(F32), out2(jnp.int32),
                   jax.ShapeDtypeStruct((n_experts, LANES), jnp.int32)],
        grid=(tokens // tm,),
        in_specs=[pl.BlockSpec((tm, d_model), lambda i: (i, 0)),
                  pl.BlockSpec((n_experts, d_model), lambda i: (0, 0)),
                  pl.BlockSpec((n_experts, 1), lambda i: (0, 0))],
        out_specs=[spec2, spec2, spec2, pl.BlockSpec((n_experts, LANES), lambda i: (0, 0))],
        scratch_shapes=[pltpu.VMEM((n_experts, 1), F32)],
        compiler_params=_params(1),
        name="router",
    )(h, router_w.T, router_bias.reshape(n_experts, 1).astype(F32))


def _row_move_kernel(src_idx_ref, dst_idx_ref, zero_row_ref, zero_on_ref, src_ref, dst_ref,
                     zero_sc, sems, *, n_rows, n_zero, segs, window):
    if n_zero:
        zero_sc[...] = jnp.zeros(zero_sc.shape, zero_sc.dtype)
        span = zero_sc.shape[0]

        def zero_copy(e):
            start = pl.multiple_of(zero_row_ref[e] * segs, span)
            return pltpu.make_async_copy(zero_sc, dst_ref.at[pl.ds(start, span)], sems.at[0])

        for e in range(n_zero):
            @pl.when(zero_on_ref[e] > 0)
            def _start(e=e):
                zero_copy(e).start()
        for e in range(n_zero):
            @pl.when(zero_on_ref[e] > 0)
            def _wait(e=e):
                zero_copy(e).wait()

    def row_copy(i):
        src = pl.multiple_of(src_idx_ref[i] * segs, segs)
        dst = pl.multiple_of(dst_idx_ref[i] * segs, segs)
        return pltpu.make_async_copy(src_ref.at[pl.ds(src, segs)], dst_ref.at[pl.ds(dst, segs)],
                                     sems.at[1])

    def prime(i, carry):
        row_copy(i).start()
        return carry

    def steady(i, carry):
        row_copy(i - window).wait()
        row_copy(i).start()
        return carry

    def drain(i, carry):
        row_copy(i).wait()
        return carry

    lax.fori_loop(0, window, prime, 0)
    lax.fori_loop(window, n_rows, steady, 0)
    lax.fori_loop(n_rows - window, n_rows, drain, 0)


def _move_rows(src, src_idx, dst_idx, n_dst, segs, zero_rows=None, zero_on=None, zero_block=1):
    n_rows = src_idx.shape[0]
    n_zero = 0 if zero_rows is None else zero_rows.shape[0]
    if zero_rows is None:
        zero_rows = jnp.zeros((1,), jnp.int32)
        zero_on = jnp.zeros((1,), jnp.int32)
    window = min(DMA_WINDOW, n_rows)
    kern = functools.partial(_row_move_kernel, n_rows=n_rows, n_zero=n_zero, segs=segs, window=window)
    return pl.pallas_call(
        kern,
        out_shape=jax.ShapeDtypeStruct((n_dst * segs, LANES), src.dtype),
        grid_spec=pltpu.PrefetchScalarGridSpec(
            num_scalar_prefetch=4, grid=(1,),
            in_specs=[pl.BlockSpec(memory_space=pl.ANY)],
            out_specs=pl.BlockSpec(memory_space=pl.ANY),
            scratch_shapes=[pltpu.VMEM((zero_block * segs, LANES), src.dtype),
                            pltpu.SemaphoreType.DMA((2,))]),
        compiler_params=_params(1),
        name="move_rows",
    )(src_idx, dst_idx, zero_rows, zero_on, src)


def _cast_kernel(x_ref, o_ref):
    o_ref[...] = x_ref[...].astype(o_ref.dtype)


def _to_bf16(w):
    n, r, c = w.shape
    tr = _tile(r, 1024)
    return pl.pallas_call(
        _cast_kernel,
        out_shape=jax.ShapeDtypeStruct(w.shape, BF16),
        grid=(n, r // tr),
        in_specs=[pl.BlockSpec((1, tr, c), lambda a, b: (a, b, 0))],
        out_specs=pl.BlockSpec((1, tr, c), lambda a, b: (a, b, 0)),
        compiler_params=_params(2),
        name="to_bf16",
    )(w)


def _expert_kernel(tile_expert_ref, tile_block_ref, n_used_ref, x_ref, wg_ref, wu_ref, wd_ref, o_ref,
                   *, d_model, tm):
    del tile_expert_ref, tile_block_ref
    half = d_model // 2

    @pl.when(pl.program_id(0) >= n_used_ref[0])
    def _unused_tile():
        o_ref[...] = jnp.zeros(o_ref.shape, o_ref.dtype)

    @pl.when(pl.program_id(0) < n_used_ref[0])
    def _compute():
        lo, hi = _unpack_bf16_pair(_slab_load(x_ref, tm, half // LANES))
        gate = (jnp.dot(lo, wg_ref[0, :half, :], preferred_element_type=F32)
                + jnp.dot(hi, wg_ref[0, half:, :], preferred_element_type=F32))
        up = (jnp.dot(lo, wu_ref[0, :half, :], preferred_element_type=F32)
              + jnp.dot(hi, wu_ref[0, half:, :], preferred_element_type=F32))
        act = (gate * jax.nn.sigmoid(gate) * up).astype(BF16)
        segs = d_model // LANES
        col = _tile(d_model, 1024)
        for c in range(d_model // col):
            y = jnp.dot(act, wd_ref[0, :, c * col:(c + 1) * col], preferred_element_type=F32)
            for s in range(col // LANES):
                o_ref[pl.ds(c * (col // LANES) + s, tm, stride=segs), :] = y[:, s * LANES:(s + 1) * LANES]


def _expert_mlp(xs, wg, wu, wd, tile_expert, tile_block, n_used, *, layer, n_experts, tm, n_tiles):
    d_model, d_ff = wg.shape[1], wg.shape[2]
    half_segs = d_model // 2 // LANES
    segs = d_model // LANES
    base = layer * n_experts
    return pl.pallas_call(
        functools.partial(_expert_kernel, d_model=d_model, tm=tm),
        out_shape=jax.ShapeDtypeStruct((n_tiles * tm * segs, LANES), F32),
        grid_spec=pltpu.PrefetchScalarGridSpec(
            num_scalar_prefetch=3, grid=(n_tiles,),
            in_specs=[pl.BlockSpec((tm * half_segs, LANES), lambda i, te, tb, nu: (tb[i], 0)),
                      pl.BlockSpec((1, d_model, d_ff), lambda i, te, tb, nu: (base + te[i], 0, 0)),
                      pl.BlockSpec((1, d_model, d_ff), lambda i, te, tb, nu: (base + te[i], 0, 0)),
                      pl.BlockSpec((1, d_ff, d_model), lambda i, te, tb, nu: (base + te[i], 0, 0))],
            out_specs=pl.BlockSpec((tm * segs, LANES), lambda i, te, tb, nu: (i, 0))),
        compiler_params=_params(1),
        name="expert_mlp",
    )(tile_expert, tile_block, n_used, xs, wg, wu, wd)


def _grouped_moe(h, hp, router_w, router_bias, wg, wu, wd, *, layer, tm):
    tokens, d_model = h.shape
    n_experts = router_w.shape[1]
    half_segs = d_model // 2 // LANES
    segs = d_model // LANES
    idx, gates, rank, counts = _route(h, router_w, router_bias)

    counts = counts[:, 0]
    padded = ((counts + tm - 1) // tm) * tm
    seg_end = jnp.cumsum(padded)
    seg_start = seg_end - padded
    n_tiles = (2 * tokens) // tm + n_experts
    slot = (jnp.take(seg_start, idx.reshape(-1)) + rank.reshape(-1)).astype(jnp.int32)
    n_used = (seg_end[-1] // tm).astype(jnp.int32)
    tile_block = jnp.minimum(jnp.arange(n_tiles, dtype=jnp.int32), n_used - 1)
    tile_expert = jnp.sum(seg_end[None, :] <= (tile_block * tm)[:, None], axis=1).astype(jnp.int32)
    tile_expert = jnp.minimum(tile_expert, n_experts - 1)
    spare = n_used + jnp.arange(n_experts, dtype=jnp.int32)
    zero_rows = jnp.concatenate([jnp.maximum(seg_end - tm, 0).astype(jnp.int32),
                                 jnp.minimum(spare, n_tiles - 1) * tm])
    zero_on = jnp.concatenate([padded > counts, spare < n_tiles]).astype(jnp.int32)

    token_of = jnp.tile(jnp.arange(tokens, dtype=jnp.int32), 2)
    xs = _move_rows(hp, token_of, slot, n_tiles * tm, half_segs, zero_rows, zero_on, zero_block=tm)
    ys = _expert_mlp(xs, wg, wu, wd, tile_expert, tile_block, n_used.reshape(1),
                     layer=layer, n_experts=n_experts, tm=tm, n_tiles=n_tiles)
    y2 = _move_rows(ys, slot, jnp.arange(2 * tokens, dtype=jnp.int32), 2 * tokens, segs)
    return y2.reshape(2, tokens * segs, LANES), gates.T


def kernel(x, w_qkv_diff, w_o_diff, lambda_q1, lambda_k1, lambda_q2, lambda_k2, subln_g, w_pool,
           pool_scale, w_qkv_sb, w_o_sb, ln_mix_g, ln_mix_b, ln_ffn_g, ln_ffn_b, router_w,
           router_bias, w_gate, w_up, w_down):
    batch, seq, d_model = x.shape
    depth = ln_mix_g.shape[0]
    n_experts = router_w.shape[1]
    d_ff = w_gate.shape[-1]
    tokens = batch * seq
    alpha = (2.0 * depth) ** 0.25
    moe_tile = _tile(2 * tokens, 256)

    wg = _to_bf16(w_gate.reshape(depth * n_experts, d_model, d_ff))
    wu = _to_bf16(w_up.reshape(depth * n_experts, d_model, d_ff))
    wd = _to_bf16(w_down.reshape(depth * n_experts, d_ff, d_model))
    cos, sin = _rope_tables(seq)

    h = x.reshape(tokens, d_model)
    hb = None
    for i in range(depth):
        kind, slot = i % N_MIXERS, i // N_MIXERS
        if kind == 1:
            mix = _pool_mixer(h, w_pool, pool_scale, slot, seq=seq)
        else:
            if hb is None:
                hb = h.astype(BF16)
            if kind == 0:
                lambda_init = 0.8 - 0.6 * math.exp(-0.3 * i)
                qkv = _qkv_project(hb, w_qkv_diff, slot, cos, sin, seq=seq, rope=True)
                o = _diff_attention(qkv, lambda_q1[slot], lambda_k1[slot], lambda_q2[slot],
                                    lambda_k2[slot], subln_g[slot], batch=batch, seq=seq,
                                    d_model=d_model, lambda_init=lambda_init)
                mix = _project(o, w_o_diff, slot, F32)
            else:
                qkv = _qkv_project(hb, w_qkv_sb, slot, cos, sin, seq=seq, rope=False)
                o = _sb_attention(qkv, batch=batch, seq=seq, d_model=d_model)
                mix = _project(o, w_o_sb, slot, F32)
        h, hp = _ln_mix(h, mix, ln_mix_g, ln_mix_b, i, alpha=alpha, emit="packed")
        y2, gates = _grouped_moe(h, hp, router_w, router_bias, wg, wu, wd, layer=i, tm=moe_tile)
        next_kind = (i + 1) % N_MIXERS
        emit = "bf16" if (i + 1 < depth and next_kind != 1) else "none"
        outs = _ln_moe(h, y2, gates, ln_ffn_g, ln_ffn_b, i, alpha=alpha, emit=emit)
        h = outs[0]
        hb = outs[1] if emit == "bf16" else None
    return h.reshape(batch, seq, d_model)
```

```python
import functools
import math

import jax
import jax.numpy as jnp
from jax import lax
from jax.experimental import pallas as pl
from jax.experimental.pallas import tpu as pltpu

F32 = jnp.float32
BF16 = jnp.bfloat16

HEAD_DIM = 128
CHUNK = 64
N_MIXERS = 3
POOL_WINDOWS = (2, 4, 8, 16)
POOL_HALO = 16
N_GROUPS = 4
ROPE_THETA = 10000.0
LN_EPS = 1e-5
RMS_EPS = 1e-5

LANES = 128
VMEM_LIMIT = 60 * 1024 * 1024
ATTN_ROWS = 512
SB_KEYS = 256


def _tile(n, pref):
    return pref if n % pref == 0 else n


def _params(n_axes, vmem=VMEM_LIMIT):
    return pltpu.CompilerParams(dimension_semantics=("arbitrary",) * n_axes, vmem_limit_bytes=vmem)


def _slab_load(ref, rows, segs, lead=()):
    return jnp.concatenate([ref[lead + (pl.ds(s, rows, stride=segs), slice(None))] for s in range(segs)],
                           axis=1)


def _slab_store(ref, value, rows, segs):
    for s in range(segs):
        ref[pl.ds(s, rows, stride=segs), :] = value[:, s * LANES:(s + 1) * LANES]


def _proj_kernel(x_ref, w_ref, o_ref, wb_ref):
    @pl.when(pl.program_id(1) == 0)
    def _cast():
        wb_ref[...] = w_ref[0].astype(BF16)

    o_ref[...] = jnp.dot(x_ref[...], wb_ref[...], preferred_element_type=F32).astype(o_ref.dtype)


def _project(x, w, slot, out_dtype):
    m, k = x.shape
    n = w.shape[2]
    tm, tn = _tile(m, 1024), _tile(n, 512)
    return pl.pallas_call(
        _proj_kernel,
        out_shape=jax.ShapeDtypeStruct((m, n), out_dtype),
        grid=(n // tn, m // tm),
        in_specs=[pl.BlockSpec((tm, k), lambda j, i: (i, 0)),
                  pl.BlockSpec((1, k, tn), lambda j, i: (slot, 0, j))],
        out_specs=pl.BlockSpec((tm, tn), lambda j, i: (i, j)),
        scratch_shapes=[pltpu.VMEM((k, tn), BF16)],
        compiler_params=_params(2),
        name="proj",
    )(x, w)


def _qkv_kernel(x_ref, w_ref, cos_ref, sin_ref, o_ref, wb_ref, *, d_model, tn, rope, scale):
    col0 = pl.program_id(0) * tn

    @pl.when(pl.program_id(1) == 0)
    def _cast():
        wb_ref[...] = w_ref[0].astype(BF16)

    acc = jnp.dot(x_ref[...], wb_ref[...], preferred_element_type=F32)
    q_scale = jnp.where(col0 < d_model, scale, 1.0).astype(F32)
    if rope:
        @pl.when(col0 < 2 * d_model)
        def _rotary():
            cos, sin = cos_ref[...], sin_ref[...]
            for c in range(tn // HEAD_DIM):
                t = acc[:, c * HEAD_DIM:(c + 1) * HEAD_DIM]
                r = (t * cos + pltpu.roll(t, HEAD_DIM // 2, axis=1) * sin) * q_scale
                o_ref[:, c * HEAD_DIM:(c + 1) * HEAD_DIM] = r.astype(o_ref.dtype)

        @pl.when(col0 >= 2 * d_model)
        def _plain():
            o_ref[...] = acc.astype(o_ref.dtype)
    else:
        o_ref[...] = (acc * q_scale).astype(o_ref.dtype)


def _qkv_project(x, w, slot, cos, sin, *, seq, rope):
    m, k = x.shape
    n = w.shape[2]
    tm, tn = _tile(seq, 1024), _tile(k, 512)
    n_pos = seq // tm
    kern = functools.partial(_qkv_kernel, d_model=k, tn=tn, rope=rope, scale=HEAD_DIM ** -0.5)
    return pl.pallas_call(
        kern,
        out_shape=jax.ShapeDtypeStruct((m, n), BF16),
        grid=(n // tn, m // tm),
        in_specs=[pl.BlockSpec((tm, k), lambda j, i: (i, 0)),
                  pl.BlockSpec((1, k, tn), lambda j, i: (slot, 0, j)),
                  pl.BlockSpec((tm, HEAD_DIM), lambda j, i: (i % n_pos, 0)),
                  pl.BlockSpec((tm, HEAD_DIM), lambda j, i: (i % n_pos, 0))],
        out_specs=pl.BlockSpec((tm, tn), lambda j, i: (i, j)),
        scratch_shapes=[pltpu.VMEM((k, tn), BF16)],
        compiler_params=_params(2),
        name="qkv_proj",
    )(x, w, cos, sin)


def _rope_tables(seq):
    half = HEAD_DIM // 2
    inv_freq = ROPE_THETA ** (-jnp.arange(half, dtype=F32) / half)
    ang = jnp.arange(seq).astype(F32)[:, None] * inv_freq[None, :]
    cos, sin = jnp.cos(ang), jnp.sin(ang)
    return jnp.concatenate([cos, cos], axis=-1), jnp.concatenate([-sin, sin], axis=-1)


def _nt_dot(a, b):
    return lax.dot_general(a, b, (((1,), (1,)), ((), ())), preferred_element_type=F32)


def _diff_attn_kernel(q1_ref, q2_ref, k1_ref, k2_ref, v_ref, lq1_ref, lk1_ref, lq2_ref, lk2_ref,
                      g_ref, o_ref, *, seq, t, lambda_init):
    lam = (jnp.exp(jnp.sum(lq1_ref[...] * lk1_ref[...], axis=-1, keepdims=True))
           - jnp.exp(jnp.sum(lq2_ref[...] * lk2_ref[...], axis=-1, keepdims=True)) + lambda_init)
    row_chunk = lax.broadcasted_iota(jnp.int32, (t, t), 0) // CHUNK
    col_chunk = lax.broadcasted_iota(jnp.int32, (t, t), 1) // CHUNK
    visible = col_chunk <= row_chunk

    for qi in range(seq // t):
        r0 = qi * t
        heads = []
        for q_ref, k_ref in ((q1_ref, k1_ref), (q2_ref, k2_ref)):
            q = q_ref[r0:r0 + t, :]
            s_diag = jnp.where(visible, _nt_dot(q, k_ref[r0:r0 + t, :]), -jnp.inf)
            m = jnp.max(s_diag, axis=-1, keepdims=True)
            if qi:
                s_past = _nt_dot(q, k_ref[0:r0, :])
                m = jnp.maximum(m, jnp.max(s_past, axis=-1, keepdims=True))
            p = jnp.exp(s_diag - m)
            denom = jnp.sum(p, axis=-1, keepdims=True)
            acc = jnp.dot(p.astype(BF16), v_ref[r0:r0 + t, :], preferred_element_type=F32)
            if qi:
                p = jnp.exp(s_past - m)
                denom = denom + jnp.sum(p, axis=-1, keepdims=True)
                acc = acc + jnp.dot(p.astype(BF16), v_ref[0:r0, :], preferred_element_type=F32)
            heads.append(acc / denom)
        o = heads[0] - lam * heads[1]
        o = o * lax.rsqrt(jnp.mean(o * o, axis=-1, keepdims=True) + RMS_EPS) * g_ref[...]
        o_ref[r0:r0 + t, :] = (o * (1.0 - lambda_init)).astype(o_ref.dtype)


def _diff_attention(qkv, lq1, lk1, lq2, lk2, subln_g, *, batch, seq, d_model, lambda_init):
    n_heads = d_model // (2 * HEAD_DIM)
    t = _tile(seq, ATTN_ROWS)
    kq = d_model // HEAD_DIM
    kv = 2 * d_model // (2 * HEAD_DIM)
    vec = lambda a: a.reshape(1, -1).astype(F32)
    small = lambda n: pl.BlockSpec((1, n), lambda b, h: (0, 0))
    kern = functools.partial(_diff_attn_kernel, seq=seq, t=t, lambda_init=lambda_init)
    return pl.pallas_call(
        kern,
        out_shape=jax.ShapeDtypeStruct((batch * seq, d_model), BF16),
        grid=(batch, n_heads),
        in_specs=[pl.BlockSpec((seq, HEAD_DIM), lambda b, h: (b, 2 * h)),
                  pl.BlockSpec((seq, HEAD_DIM), lambda b, h: (b, 2 * h + 1)),
                  pl.BlockSpec((seq, HEAD_DIM), lambda b, h: (b, kq + 2 * h)),
                  pl.BlockSpec((seq, HEAD_DIM), lambda b, h: (b, kq + 2 * h + 1)),
                  pl.BlockSpec((seq, 2 * HEAD_DIM), lambda b, h: (b, kv + h)),
                  small(HEAD_DIM), small(HEAD_DIM), small(HEAD_DIM), small(HEAD_DIM),
                  small(2 * HEAD_DIM)],
        out_specs=pl.BlockSpec((seq, 2 * HEAD_DIM), lambda b, h: (b, h)),
        compiler_params=_params(2),
        name="diff_attn",
    )(qkv, qkv, qkv, qkv, qkv, vec(lq1), vec(lk1), vec(lq2), vec(lk2), vec(subln_g))


def _sb_attn_kernel(q_ref, k_ref, v_ref, o_ref, *, seq, t, kb):
    src = lax.broadcasted_iota(jnp.int32, (kb, kb + LANES), 0)
    dst = lax.broadcasted_iota(jnp.int32, (kb, kb + LANES), 1)
    after = jnp.where((src > dst) | (dst >= kb), 1.0, 0.0).astype(BF16)
    row = lax.broadcasted_iota(jnp.int32, (t, kb), 0)
    col = lax.broadcasted_iota(jnp.int32, (t, kb), 1)

    for qi in range(seq // t):
        r0 = qi * t
        q = q_ref[r0:r0 + t, :]
        run = jnp.zeros((t, LANES), F32)
        acc = jnp.zeros((t, HEAD_DIM), F32)
        for j in range((r0 + t) // kb - 1, -1, -1):
            c0 = j * kb
            z = _nt_dot(q, k_ref[c0:c0 + kb, :])
            log1p_e = jnp.log(1.0 + jnp.exp(-jnp.abs(z)))
            log_keep = -(jnp.maximum(z, 0.0) + log1p_e)
            on_diagonal = c0 + kb > r0
            if on_diagonal:
                strict = col - row < r0 - c0
                log_keep = jnp.where(strict, log_keep, 0.0)
            hi = log_keep.astype(BF16)
            lo = (log_keep - hi.astype(F32)).astype(BF16)
            sums = (jnp.dot(hi, after, preferred_element_type=F32)
                    + jnp.dot(lo, after, preferred_element_type=F32))
            later = sums[:, :kb] + jnp.concatenate([run] * (kb // LANES), axis=1)
            w = jnp.exp((jnp.minimum(z, 0.0) - log1p_e) + later)
            if on_diagonal:
                w = jnp.where(strict, w, 0.0)
            acc = acc + jnp.dot(w.astype(BF16), v_ref[c0:c0 + kb, :], preferred_element_type=F32)
            run = run + sums[:, kb:]
        o_ref[r0:r0 + t, :] = acc.astype(o_ref.dtype)


def _sb_attention(qkv, *, batch, seq, d_model):
    n_heads = d_model // HEAD_DIM
    t = _tile(seq, ATTN_ROWS)
    kb = _tile(t, SB_KEYS)
    blk = lambda first: pl.BlockSpec((seq, HEAD_DIM), lambda b, h: (b, first + h))
    return pl.pallas_call(
        functools.partial(_sb_attn_kernel, seq=seq, t=t, kb=kb),
        out_shape=jax.ShapeDtypeStruct((batch * seq, d_model), BF16),
        grid=(batch, n_heads),
        in_specs=[blk(0), blk(n_heads), blk(2 * n_heads)],
        out_specs=blk(0),
        compiler_params=_params(2),
        name="sb_attn",
    )(qkv, qkv, qkv)


def _pool_kernel(x_ref, halo_ref, w_ref, scale_ref, o_ref, wb_ref, *, tm, seq):
    g = pl.program_id(0)
    i = pl.program_id(1)

    @pl.when(i == 0)
    def _cast():
        wb_ref[...] = w_ref[0, 0].astype(BF16)

    pos0 = (i * tm) % seq
    x = x_ref[...]
    halo = halo_ref[...] * jnp.where(pos0 == 0, 0.0, 1.0).astype(F32)
    xe = jnp.concatenate([halo, x], axis=0)
    pos = pos0 + lax.broadcasted_iota(jnp.int32, (tm, 1), 0)

    for gi, window in enumerate(POOL_WINDOWS):
        @pl.when(g == gi)
        def _window(window=window):
            cur = xe
            span = 1
            while span < window:
                cur = cur + pltpu.roll(cur, span, axis=0)
                span *= 2
            count = jnp.minimum(pos + 1, window).astype(F32)
            pooled = cur[POOL_HALO:, :] / count - x
            y = jnp.dot(pooled.astype(BF16), wb_ref[...], preferred_element_type=F32)
            o_ref[...] = y * scale_ref[0]


def _pool_mixer(h, w_pool, pool_scale, slot, *, seq):
    tokens, d_model = h.shape
    groups, width = w_pool.shape[1], w_pool.shape[2]
    tm = _tile(seq, 512)
    halo_blocks = tm // POOL_HALO
    return pl.pallas_call(
        functools.partial(_pool_kernel, tm=tm, seq=seq),
        out_shape=jax.ShapeDtypeStruct((tokens, d_model), F32),
        grid=(groups, tokens // tm),
        in_specs=[pl.BlockSpec((tm, width), lambda g, i: (i, g)),
                  pl.BlockSpec((POOL_HALO, width), lambda g, i: (jnp.maximum(i * halo_blocks - 1, 0), g)),
                  pl.BlockSpec((1, 1, width, width), lambda g, i: (slot, g, 0, 0)),
                  pl.BlockSpec((1, 1, width), lambda g, i: (slot, 0, g))],
        out_specs=pl.BlockSpec((tm, width), lambda g, i: (i, g)),
        scratch_shapes=[pltpu.VMEM((width, width), BF16)],
        compiler_params=_params(2),
        name="pool_mixer",
    )(h, h, w_pool, pool_scale.reshape(pool_scale.shape[0], 1, d_model))


def _pack_bf16_pair(lo, hi):
    lo_bits = pltpu.bitcast(lo.astype(BF16).astype(F32), jnp.uint32) >> 16
    hi_bits = pltpu.bitcast(hi.astype(BF16).astype(F32), jnp.uint32) & jnp.uint32(0xFFFF0000)
    return lo_bits | hi_bits


def _unpack_bf16_pair(packed):
    lo = pltpu.bitcast(packed << 16, F32).astype(BF16)
    hi = pltpu.bitcast(packed & jnp.uint32(0xFFFF0000), F32).astype(BF16)
    return lo, hi


def _layer_norm_rows(y, g, b):
    mu = jnp.mean(y, axis=-1, keepdims=True)
    yc = y - mu
    var = jnp.mean(yc * yc, axis=-1, keepdims=True)
    return yc * lax.rsqrt(var + LN_EPS) * g + b


def _emit_norm(out, o_refs, emit):
    o_refs[0][...] = out
    if emit == "bf16":
        o_refs[1][...] = out.astype(BF16)
    elif emit == "packed":
        rows, half = out.shape[0], out.shape[1] // 2
        _slab_store(o_refs[1], _pack_bf16_pair(out[:, :half], out[:, half:]), rows, half // LANES)


def _ln_mix_kernel(h_ref, a_ref, g_ref, b_ref, *o_refs, alpha, emit):
    y = alpha * h_ref[...] + a_ref[...].astype(F32)
    _emit_norm(_layer_norm_rows(y, g_ref[0], b_ref[0]), o_refs, emit)


def _ln_moe_kernel(slot_ref, h_ref, ys_ref, gate_ref, g_ref, b_ref, *rest, alpha, emit, tokens):
    *o_refs, ybuf, sems = rest
    rows, d_model = h_ref.shape
    segs = d_model // LANES
    i = pl.program_id(0)
    n_steps = pl.num_programs(0)

    def gather(step, buf):
        def body(r, carry):
            for k in range(2):
                src = pl.multiple_of(slot_ref[k * tokens + step * rows + r] * segs, segs)
                dst = pl.multiple_of(r * segs, segs)
                pltpu.make_async_copy(ys_ref.at[pl.ds(src, segs)], ybuf.at[buf, k, pl.ds(dst, segs)],
                                      sems.at[buf]).start()
            return carry
        lax.fori_loop(0, rows, body, 0, unroll=4)

    @pl.when(i == 0)
    def _first():
        gather(0, 0)

    @pl.when(i + 1 < n_steps)
    def _ahead():
        gather(i + 1, (i + 1) % 2)

    buf = i % 2
    for k in range(2):
        pltpu.make_async_copy(ys_ref.at[pl.ds(0, rows * segs)], ybuf.at[buf, k], sems.at[buf]).wait()

    gates = gate_ref[...]
    ffn = (gates[:, 0:1] * _slab_load(ybuf, rows, segs, lead=(buf, 0))
           + gates[:, 1:2] * _slab_load(ybuf, rows, segs, lead=(buf, 1)))
    y = alpha * h_ref[...] + ffn
    _emit_norm(_layer_norm_rows(y, g_ref[0], b_ref[0]), o_refs, emit)


def _norm_outputs(tokens, d_model, tm, emit):
    shapes = [jax.ShapeDtypeStruct((tokens, d_model), F32)]
    specs = [pl.BlockSpec((tm, d_model), lambda i, *_: (i, 0))]
    if emit == "bf16":
        shapes.append(jax.ShapeDtypeStruct((tokens, d_model), BF16))
        specs.append(pl.BlockSpec((tm, d_model), lambda i, *_: (i, 0)))
    elif emit == "packed":
        segs = d_model // 2 // LANES
        shapes.append(jax.ShapeDtypeStruct((tokens * segs, LANES), jnp.uint32))
        specs.append(pl.BlockSpec((tm * segs, LANES), lambda i, *_: (i, 0)))
    return shapes, specs


def _ln_mix(h, mix, gain, bias, layer, *, alpha, emit):
    tokens, d_model = h.shape
    tm = _tile(tokens, 256)
    shapes, specs = _norm_outputs(tokens, d_model, tm, emit)
    row = pl.BlockSpec((tm, d_model), lambda i: (i, 0))
    vec = pl.BlockSpec((1, 1, d_model), lambda i: (layer, 0, 0))
    depth = gain.shape[0]
    return pl.pallas_call(
        functools.partial(_ln_mix_kernel, alpha=alpha, emit=emit),
        out_shape=shapes, grid=(tokens // tm,),
        in_specs=[row, row, vec, vec], out_specs=specs,
        compiler_params=_params(1), name="ln_mix",
    )(h, mix, gain.reshape(depth, 1, d_model), bias.reshape(depth, 1, d_model))


def _ln_moe(h, ys, slot, gates, gain, bias, layer, *, alpha, emit):
    tokens, d_model = h.shape
    tm = _tile(tokens, 256)
    segs = d_model // LANES
    shapes, specs = _norm_outputs(tokens, d_model, tm, emit)
    row = pl.BlockSpec((tm, d_model), lambda i, s: (i, 0))
    vec = pl.BlockSpec((1, 1, d_model), lambda i, s: (layer, 0, 0))
    depth = gain.shape[0]
    return pl.pallas_call(
        functools.partial(_ln_moe_kernel, alpha=alpha, emit=emit, tokens=tokens),
        out_shape=shapes,
        grid_spec=pltpu.PrefetchScalarGridSpec(
            num_scalar_prefetch=1, grid=(tokens // tm,),
            in_specs=[row, pl.BlockSpec(memory_space=pl.ANY),
                      pl.BlockSpec((tm, 2), lambda i, s: (i, 0)), vec, vec],
            out_specs=specs,
            scratch_shapes=[pltpu.VMEM((2, 2, tm * segs, LANES), F32),
                            pltpu.SemaphoreType.DMA((2,))]),
        compiler_params=_params(1), name="ln_moe",
    )(slot, h, ys, gates, gain.reshape(depth, 1, d_model), bias.reshape(depth, 1, d_model))


def _first_argmax(vals):
    best = vals[0]
    for v in vals[1:]:
        best = jnp.maximum(best, v)
    idx = jnp.full(best.shape, len(vals) - 1, jnp.int32)
    for j in range(len(vals) - 2, -1, -1):
        idx = jnp.where(vals[j] == best, j, idx)
    return best, idx


def _router_kernel(h_ref, rw_ref, bias_ref, idx_ref, gate_ref, rank_ref, count_ref, run_sc,
                   *, tm, n_experts):
    per_group = n_experts // N_GROUPS

    @pl.when(pl.program_id(0) == 0)
    def _init():
        run_sc[...] = jnp.zeros(run_sc.shape, F32)

    logits = lax.dot_general(rw_ref[...], h_ref[...], (((1,), (1,)), ((), ())),
                             precision=lax.Precision.HIGHEST, preferred_element_type=F32)
    aff = jax.nn.sigmoid(logits)
    sel = aff + bias_ref[...]
    rows = [sel[e:e + 1, :] for e in range(n_experts)]
    aff_rows = [aff[e:e + 1, :] for e in range(n_experts)]

    scores = []
    for g in range(N_GROUPS):
        a, b, c, d = rows[g * per_group:(g + 1) * per_group]
        lo1, hi1 = jnp.minimum(a, b), jnp.maximum(a, b)
        lo2, hi2 = jnp.minimum(c, d), jnp.maximum(c, d)
        scores.append(jnp.maximum(hi1, hi2) + jnp.maximum(jnp.minimum(hi1, hi2), jnp.maximum(lo1, lo2)))
    _, grp = _first_argmax(scores)

    def in_group(table, j):
        out = table[(N_GROUPS - 1) * per_group + j]
        for g in range(N_GROUPS - 2, -1, -1):
            out = jnp.where(grp == g, table[g * per_group + j], out)
        return out

    cand = [in_group(rows, j) for j in range(per_group)]
    cand_aff = [in_group(aff_rows, j) for j in range(per_group)]
    _, loc0 = _first_argmax(cand)
    rest = [jnp.where(loc0 == j, -jnp.inf, cand[j]) for j in range(per_group)]
    _, loc1 = _first_argmax(rest)

    def pick(table, loc):
        out = table[per_group - 1]
        for j in range(per_group - 2, -1, -1):
            out = jnp.where(loc == j, table[j], out)
        return out

    g0, g1 = pick(cand_aff, loc0), pick(cand_aff, loc1)
    total = g0 + g1
    e0 = grp * per_group + loc0
    e1 = grp * per_group + loc1
    idx_ref[0:1, :] = e0
    idx_ref[1:2, :] = e1
    gate_ref[0:1, :] = g0 / total
    gate_ref[1:2, :] = g1 / total

    expert = lax.broadcasted_iota(jnp.int32, (n_experts, tm), 0)
    hit0 = expert == e0
    hit1 = expert == e1
    chosen = jnp.where(hit0, 1.0, jnp.where(hit1, 1.0, 0.0))
    earlier = jnp.where(lax.broadcasted_iota(jnp.int32, (tm, tm), 0)
                        < lax.broadcasted_iota(jnp.int32, (tm, tm), 1), 1.0, 0.0).astype(BF16)
    before = jnp.dot(chosen.astype(BF16), earlier, preferred_element_type=F32) + run_sc[...]
    rank_ref[0:1, :] = jnp.sum(jnp.where(hit0, before, 0.0), axis=0, keepdims=True).astype(jnp.int32)
    rank_ref[1:2, :] = jnp.sum(jnp.where(hit1, before, 0.0), axis=0, keepdims=True).astype(jnp.int32)
    run = run_sc[...] + jnp.sum(chosen, axis=1, keepdims=True)
    run_sc[...] = run
    count_ref[...] = jnp.broadcast_to(run, count_ref.shape).astype(jnp.int32)


def _route(h, router_w, router_bias):
    tokens, d_model = h.shape
    n_experts = router_w.shape[1]
    tm = _tile(tokens, 512)
    out2 = lambda dt: jax.ShapeDtypeStruct((2, tokens), dt)
    spec2 = pl.BlockSpec((2, tm), lambda i: (0, i))
    return pl.pallas_call(
        functools.partial(_router_kernel, tm=tm, n_experts=n_experts),
        out_shape=[out2(jnp.int32), out2(F32), out2(jnp.int32),
                   jax.ShapeDtypeStruct((n_experts, LANES), jnp.int32)],
        grid=(tokens // tm,),
        in_specs=[pl.BlockSpec((tm, d_model), lambda i: (i, 0)),
                  pl.BlockSpec((n_experts, d_model), lambda i: (0, 0)),
                  pl.BlockSpec((n_experts, 1), lambda i: (0, 0))],
        out_specs=[spec2, spec2, spec2, pl.BlockSpec((n_experts, LANES), lambda i: (0, 0))],
        scratch_shapes=[pltpu.VMEM((n_experts, 1), F32)],
        compiler_params=_params(1),
        name="router",
    )(h, router_w.T, router_bias.reshape(n_experts, 1).astype(F32))


def _dispatch_kernel(slot_ref, zero_row_ref, zero_on_ref, x_ref, dst_ref, zero_sc, sems,
                     *, tokens, rows, n_zero, segs):
    @pl.when(pl.program_id(0) == 0)
    def _zero_fill():
        zero_sc[...] = jnp.zeros(zero_sc.shape, zero_sc.dtype)
        span = zero_sc.shape[0]

        def zero_copy(e):
            start = pl.multiple_of(zero_row_ref[e] * segs, span)
            return pltpu.make_async_copy(zero_sc, dst_ref.at[pl.ds(start, span)], sems.at[0])

        for e in range(n_zero):
            @pl.when(zero_on_ref[e] > 0)
            def _start(e=e):
                zero_copy(e).start()
        for e in range(n_zero):
            @pl.when(zero_on_ref[e] > 0)
            def _wait(e=e):
                zero_copy(e).wait()

    base = pl.program_id(0) * rows

    def body(r, carry):
        src = pl.multiple_of(r * segs, segs)
        for k in range(2):
            dst = pl.multiple_of(slot_ref[k * tokens + base + r] * segs, segs)
            pltpu.make_async_copy(x_ref.at[pl.ds(src, segs)], dst_ref.at[pl.ds(dst, segs)],
                                  sems.at[1]).start()
        return carry

    lax.fori_loop(0, rows, body, 0, unroll=4)
    for k in range(2):
        pltpu.make_async_copy(x_ref, dst_ref.at[pl.ds(0, rows * segs)], sems.at[1]).wait()


def _dispatch(hp, slot, zero_rows, zero_on, *, tokens, n_slots, segs, tm):
    rows = _tile(tokens, 256)
    kern = functools.partial(_dispatch_kernel, tokens=tokens, rows=rows, n_zero=zero_rows.shape[0], segs=segs)
    return pl.pallas_call(
        kern,
        out_shape=jax.ShapeDtypeStruct((n_slots * segs, LANES), hp.dtype),
        grid_spec=pltpu.PrefetchScalarGridSpec(
            num_scalar_prefetch=3, grid=(tokens // rows,),
            in_specs=[pl.BlockSpec((rows * segs, LANES), lambda i, *_: (i, 0))],
            out_specs=pl.BlockSpec(memory_space=pl.ANY),
            scratch_shapes=[pltpu.VMEM((tm * segs, LANES), hp.dtype),
                            pltpu.SemaphoreType.DMA((2,))]),
        compiler_params=_params(1),
        name="dispatch",
    )(slot, zero_rows, zero_on, hp)


def _cast_kernel(x_ref, o_ref):
    o_ref[...] = x_ref[...].astype(o_ref.dtype)


def _to_bf16(w):
    n, r, c = w.shape
    tr = _tile(r, 1024)
    return pl.pallas_call(
        _cast_kernel,
        out_shape=jax.ShapeDtypeStruct(w.shape, BF16),
        grid=(n, r // tr),
        in_specs=[pl.BlockSpec((1, tr, c), lambda a, b: (a, b, 0))],
        out_specs=pl.BlockSpec((1, tr, c), lambda a, b: (a, b, 0)),
        compiler_params=_params(2),
        name="to_bf16",
    )(w)


def _expert_kernel(tile_expert_ref, tile_block_ref, n_used_ref, x_ref, wg_ref, wu_ref, wd_ref, o_ref,
                   *, d_model, tm):
    del tile_expert_ref, tile_block_ref
    half = d_model // 2

    @pl.when(pl.program_id(0) >= n_used_ref[0])
    def _unused_tile():
        o_ref[...] = jnp.zeros(o_ref.shape, o_ref.dtype)

    @pl.when(pl.program_id(0) < n_used_ref[0])
    def _compute():
        lo, hi = _unpack_bf16_pair(_slab_load(x_ref, tm, half // LANES))
        gate = (jnp.dot(lo, wg_ref[0, :half, :], preferred_element_type=F32)
                + jnp.dot(hi, wg_ref[0, half:, :], preferred_element_type=F32))
        up = (jnp.dot(lo, wu_ref[0, :half, :], preferred_element_type=F32)
              + jnp.dot(hi, wu_ref[0, half:, :], preferred_element_type=F32))
        act = (gate * jax.nn.sigmoid(gate) * up).astype(BF16)
        segs = d_model // LANES
        col = _tile(d_model, 1024)
        for c in range(d_model // col):
            y = jnp.dot(act, wd_ref[0, :, c * col:(c + 1) * col], preferred_element_type=F32)
            for s in range(col // LANES):
                o_ref[pl.ds(c * (col // LANES) + s, tm, stride=segs), :] = y[:, s * LANES:(s + 1) * LANES]


def _expert_mlp(xs, wg, wu, wd, tile_expert, tile_block, n_used, *, layer, n_experts, tm, n_tiles):
    d_model, d_ff = wg.shape[1], wg.shape[2]
    half_segs = d_model // 2 // LANES
    segs = d_model // LANES
    base = layer * n_experts
    return pl.pallas_call(
        functools.partial(_expert_kernel, d_model=d_model, tm=tm),
        out_shape=jax.ShapeDtypeStruct((n_tiles * tm * segs, LANES), F32),
        grid_spec=pltpu.PrefetchScalarGridSpec(
            num_scalar_prefetch=3, grid=(n_tiles,),
            in_specs=[pl.BlockSpec((tm * half_segs, LANES), lambda i, te, tb, nu: (tb[i], 0)),
                      pl.BlockSpec((1, d_model, d_ff), lambda i, te, tb, nu: (base + te[i], 0, 0)),
                      pl.BlockSpec((1, d_model, d_ff), lambda i, te, tb, nu: (base + te[i], 0, 0)),
                      pl.BlockSpec((1, d_ff, d_model), lambda i, te, tb, nu: (base + te[i], 0, 0))],
            out_specs=pl.BlockSpec((tm * segs, LANES), lambda i, te, tb, nu: (i, 0))),
        compiler_params=_params(1),
        name="expert_mlp",
    )(tile_expert, tile_block, n_used, xs, wg, wu, wd)


def _grouped_moe(h, hp, router_w, router_bias, wg, wu, wd, *, layer, tm):
    tokens, d_model = h.shape
    n_experts = router_w.shape[1]
    half_segs = d_model // 2 // LANES
    segs = d_model // LANES
    idx, gates, rank, counts = _route(h, router_w, router_bias)

    counts = counts[:, 0]
    padded = ((counts + tm - 1) // tm) * tm
    seg_end = jnp.cumsum(padded)
    seg_start = seg_end - padded
    n_tiles = (2 * tokens) // tm + n_experts
    slot = (jnp.take(seg_start, idx.reshape(-1)) + rank.reshape(-1)).astype(jnp.int32)
    n_used = (seg_end[-1] // tm).astype(jnp.int32)
    tile_block = jnp.minimum(jnp.arange(n_tiles, dtype=jnp.int32), n_used - 1)
    tile_expert = jnp.sum(seg_end[None, :] <= (tile_block * tm)[:, None], axis=1).astype(jnp.int32)
    tile_expert = jnp.minimum(tile_expert, n_experts - 1)
    spare = n_used + jnp.arange(n_experts, dtype=jnp.int32)
    zero_rows = jnp.concatenate([jnp.maximum(seg_end - tm, 0).astype(jnp.int32),
                                 jnp.minimum(spare, n_tiles - 1) * tm])
    zero_on = jnp.concatenate([padded > counts, spare < n_tiles]).astype(jnp.int32)

    xs = _dispatch(hp, slot, zero_rows, zero_on, tokens=tokens, n_slots=n_tiles * tm,
                   segs=half_segs, tm=tm)
    ys = _expert_mlp(xs, wg, wu, wd, tile_expert, tile_block, n_used.reshape(1),
                     layer=layer, n_experts=n_experts, tm=tm, n_tiles=n_tiles)
    return ys, slot, gates.T


def kernel(x, w_qkv_diff, w_o_diff, lambda_q1, lambda_k1, lambda_q2, lambda_k2, subln_g, w_pool,
           pool_scale, w_qkv_sb, w_o_sb, ln_mix_g, ln_mix_b, ln_ffn_g, ln_ffn_b, router_w,
           router_bias, w_gate, w_up, w_down):
    batch, seq, d_model = x.shape
    depth = ln_mix_g.shape[0]
    n_experts = router_w.shape[1]
    d_ff = w_gate.shape[-1]
    tokens = batch * seq
    alpha = (2.0 * depth) ** 0.25
    moe_tile = _tile(2 * tokens, 256)

    wg = _to_bf16(w_gate.reshape(depth * n_experts, d_model, d_ff))
    wu = _to_bf16(w_up.reshape(depth * n_experts, d_model, d_ff))
    wd = _to_bf16(w_down.reshape(depth * n_experts, d_ff, d_model))
    cos, sin = _rope_tables(seq)

    h = x.reshape(tokens, d_model)
    hb = None
    for i in range(depth):
        kind, slot = i % N_MIXERS, i // N_MIXERS
        if kind == 1:
            mix = _pool_mixer(h, w_pool, pool_scale, slot, seq=seq)
        else:
            if hb is None:
                hb = h.astype(BF16)
            if kind == 0:
                lambda_init = 0.8 - 0.6 * math.exp(-0.3 * i)
                qkv = _qkv_project(hb, w_qkv_diff, slot, cos, sin, seq=seq, rope=True)
                o = _diff_attention(qkv, lambda_q1[slot], lambda_k1[slot], lambda_q2[slot],
                                    lambda_k2[slot], subln_g[slot], batch=batch, seq=seq,
                                    d_model=d_model, lambda_init=lambda_init)
                mix = _project(o, w_o_diff, slot, F32)
            else:
                qkv = _qkv_project(hb, w_qkv_sb, slot, cos, sin, seq=seq, rope=False)
                o = _sb_attention(qkv, batch=batch, seq=seq, d_model=d_model)
                mix = _project(o, w_o_sb, slot, F32)
        h, hp = _ln_mix(h, mix, ln_mix_g, ln_mix_b, i, alpha=alpha, emit="packed")
        ys, slot_of, gates = _grouped_moe(h, hp, router_w, router_bias, wg, wu, wd, layer=i, tm=moe_tile)
        next_kind = (i + 1) % N_MIXERS
        emit = "bf16" if (i + 1 < depth and next_kind != 1) else "none"
        outs = _ln_moe(h, ys, slot_of, gates, ln_ffn_g, ln_ffn_b, i, alpha=alpha, emit=emit)
        h = outs[0]
        hb = outs[1] if emit == "bf16" else None
    return h.reshape(batch, seq, d_model)
```

```python
import functools
import math

import jax
import jax.numpy as jnp
from jax import lax
from jax.experimental import pallas as pl
from jax.experimental.pallas import tpu as pltpu

F32 = jnp.float32
BF16 = jnp.bfloat16

HEAD_DIM = 128
CHUNK = 64
N_MIXERS = 3
POOL_WINDOWS = (2, 4, 8, 16)
POOL_HALO = 16
N_GROUPS = 4
ROPE_THETA = 10000.0
LN_EPS = 1e-5
RMS_EPS = 1e-5

LANES = 128
VMEM_LIMIT = 60 * 1024 * 1024
ATTN_ROWS = 512
SB_KEYS = 256
EXP_UNDERFLOW = -104.0


def _tile(n, pref):
    return pref if n % pref == 0 else n


def _params(n_axes, vmem=VMEM_LIMIT):
    return pltpu.CompilerParams(dimension_semantics=("arbitrary",) * n_axes, vmem_limit_bytes=vmem)


def _slab_load(ref, rows, segs, lead=()):
    return jnp.concatenate([ref[lead + (pl.ds(s, rows, stride=segs), slice(None))] for s in range(segs)],
                           axis=1)


def _slab_store(ref, value, rows, segs):
    for s in range(segs):
        ref[pl.ds(s, rows, stride=segs), :] = value[:, s * LANES:(s + 1) * LANES]


def _proj_kernel(x_ref, w_ref, o_ref, wb_ref):
    @pl.when(pl.program_id(1) == 0)
    def _cast():
        wb_ref[...] = w_ref[0].astype(BF16)

    o_ref[...] = jnp.dot(x_ref[...], wb_ref[...], preferred_element_type=F32).astype(o_ref.dtype)


def _project(x, w, slot, out_dtype):
    m, k = x.shape
    n = w.shape[2]
    tm, tn = _tile(m, 1024), _tile(n, 512)
    return pl.pallas_call(
        _proj_kernel,
        out_shape=jax.ShapeDtypeStruct((m, n), out_dtype),
        grid=(n // tn, m // tm),
        in_specs=[pl.BlockSpec((tm, k), lambda j, i: (i, 0)),
                  pl.BlockSpec((1, k, tn), lambda j, i: (slot, 0, j))],
        out_specs=pl.BlockSpec((tm, tn), lambda j, i: (i, j)),
        scratch_shapes=[pltpu.VMEM((k, tn), BF16)],
        compiler_params=_params(2),
        name="proj",
    )(x, w)


def _qkv_kernel(x_ref, w_ref, cos_ref, sin_ref, o_ref, wb_ref, *, d_model, tn, rope, scale):
    col0 = pl.program_id(0) * tn

    @pl.when(pl.program_id(1) == 0)
    def _cast():
        wb_ref[...] = w_ref[0].astype(BF16)

    acc = jnp.dot(x_ref[...], wb_ref[...], preferred_element_type=F32)
    q_scale = jnp.where(col0 < d_model, scale, 1.0).astype(F32)
    if rope:
        @pl.when(col0 < 2 * d_model)
        def _rotary():
            cos, sin = cos_ref[...], sin_ref[...]
            for c in range(tn // HEAD_DIM):
                t = acc[:, c * HEAD_DIM:(c + 1) * HEAD_DIM]
                r = (t * cos + pltpu.roll(t, HEAD_DIM // 2, axis=1) * sin) * q_scale
                o_ref[:, c * HEAD_DIM:(c + 1) * HEAD_DIM] = r.astype(o_ref.dtype)

        @pl.when(col0 >= 2 * d_model)
        def _plain():
            o_ref[...] = acc.astype(o_ref.dtype)
    else:
        o_ref[...] = (acc * q_scale).astype(o_ref.dtype)


def _qkv_project(x, w, slot, cos, sin, *, seq, rope):
    m, k = x.shape
    n = w.shape[2]
    tm, tn = _tile(seq, 1024), _tile(k, 512)
    n_pos = seq // tm
    kern = functools.partial(_qkv_kernel, d_model=k, tn=tn, rope=rope, scale=HEAD_DIM ** -0.5)
    return pl.pallas_call(
        kern,
        out_shape=jax.ShapeDtypeStruct((m, n), BF16),
        grid=(n // tn, m // tm),
        in_specs=[pl.BlockSpec((tm, k), lambda j, i: (i, 0)),
                  pl.BlockSpec((1, k, tn), lambda j, i: (slot, 0, j)),
                  pl.BlockSpec((tm, HEAD_DIM), lambda j, i: (i % n_pos, 0)),
                  pl.BlockSpec((tm, HEAD_DIM), lambda j, i: (i % n_pos, 0))],
        out_specs=pl.BlockSpec((tm, tn), lambda j, i: (i, j)),
        scratch_shapes=[pltpu.VMEM((k, tn), BF16)],
        compiler_params=_params(2),
        name="qkv_proj",
    )(x, w, cos, sin)


def _rope_tables(seq):
    half = HEAD_DIM // 2
    inv_freq = ROPE_THETA ** (-jnp.arange(half, dtype=F32) / half)
    ang = jnp.arange(seq).astype(F32)[:, None] * inv_freq[None, :]
    cos, sin = jnp.cos(ang), jnp.sin(ang)
    return jnp.concatenate([cos, cos], axis=-1), jnp.concatenate([-sin, sin], axis=-1)


def _nt_dot(a, b):
    return lax.dot_general(a, b, (((1,), (1,)), ((), ())), preferred_element_type=F32)


def _diff_attn_kernel(q1_ref, q2_ref, k1_ref, k2_ref, v_ref, lq1_ref, lk1_ref, lq2_ref, lk2_ref,
                      g_ref, o_ref, *, seq, t, lambda_init):
    lam = (jnp.exp(jnp.sum(lq1_ref[...] * lk1_ref[...], axis=-1, keepdims=True))
           - jnp.exp(jnp.sum(lq2_ref[...] * lk2_ref[...], axis=-1, keepdims=True)) + lambda_init)
    row_chunk = lax.broadcasted_iota(jnp.int32, (t, t), 0) // CHUNK
    col_chunk = lax.broadcasted_iota(jnp.int32, (t, t), 1) // CHUNK
    visible = col_chunk <= row_chunk

    for qi in range(seq // t):
        r0 = qi * t
        heads = []
        for q_ref, k_ref in ((q1_ref, k1_ref), (q2_ref, k2_ref)):
            q = q_ref[r0:r0 + t, :]
            s_diag = jnp.where(visible, _nt_dot(q, k_ref[r0:r0 + t, :]), -jnp.inf)
            m = jnp.max(s_diag, axis=-1, keepdims=True)
            if qi:
                s_past = _nt_dot(q, k_ref[0:r0, :])
                m = jnp.maximum(m, jnp.max(s_past, axis=-1, keepdims=True))
            p = jnp.exp(s_diag - m)
            denom = jnp.sum(p, axis=-1, keepdims=True)
            acc = jnp.dot(p.astype(BF16), v_ref[r0:r0 + t, :], preferred_element_type=F32)
            if qi:
                p = jnp.exp(s_past - m)
                denom = denom + jnp.sum(p, axis=-1, keepdims=True)
                acc = acc + jnp.dot(p.astype(BF16), v_ref[0:r0, :], preferred_element_type=F32)
            heads.append(acc / denom)
        o = heads[0] - lam * heads[1]
        o = o * lax.rsqrt(jnp.mean(o * o, axis=-1, keepdims=True) + RMS_EPS) * g_ref[...]
        o_ref[r0:r0 + t, :] = (o * (1.0 - lambda_init)).astype(o_ref.dtype)


def _diff_attention(qkv, lq1, lk1, lq2, lk2, subln_g, *, batch, seq, d_model, lambda_init):
    n_heads = d_model // (2 * HEAD_DIM)
    t = _tile(seq, ATTN_ROWS)
    kq = d_model // HEAD_DIM
    kv = 2 * d_model // (2 * HEAD_DIM)
    vec = lambda a: a.reshape(1, -1).astype(F32)
    small = lambda n: pl.BlockSpec((1, n), lambda b, h: (0, 0))
    kern = functools.partial(_diff_attn_kernel, seq=seq, t=t, lambda_init=lambda_init)
    return pl.pallas_call(
        kern,
        out_shape=jax.ShapeDtypeStruct((batch * seq, d_model), BF16),
        grid=(batch, n_heads),
        in_specs=[pl.BlockSpec((seq, HEAD_DIM), lambda b, h: (b, 2 * h)),
                  pl.BlockSpec((seq, HEAD_DIM), lambda b, h: (b, 2 * h + 1)),
                  pl.BlockSpec((seq, HEAD_DIM), lambda b, h: (b, kq + 2 * h)),
                  pl.BlockSpec((seq, HEAD_DIM), lambda b, h: (b, kq + 2 * h + 1)),
                  pl.BlockSpec((seq, 2 * HEAD_DIM), lambda b, h: (b, kv + h)),
                  small(HEAD_DIM), small(HEAD_DIM), small(HEAD_DIM), small(HEAD_DIM),
                  small(2 * HEAD_DIM)],
        out_specs=pl.BlockSpec((seq, 2 * HEAD_DIM), lambda b, h: (b, h)),
        compiler_params=_params(2),
        name="diff_attn",
    )(qkv, qkv, qkv, qkv, qkv, vec(lq1), vec(lk1), vec(lq2), vec(lk2), vec(subln_g))


def _sb_attn_kernel(q_ref, k_ref, v_ref, o_ref, *, seq, t, kb):
    src = lax.broadcasted_iota(jnp.int32, (kb, kb + LANES), 0)
    dst = lax.broadcasted_iota(jnp.int32, (kb, kb + LANES), 1)
    after = jnp.where((src > dst) | (dst >= kb), 1.0, 0.0).astype(BF16)
    row = lax.broadcasted_iota(jnp.int32, (t, kb), 0)
    col = lax.broadcasted_iota(jnp.int32, (t, kb), 1)

    def key_block(q, r0, j, acc, run):
        c0 = j * kb
        z = _nt_dot(q, k_ref[c0:c0 + kb, :])
        log1p_e = jnp.log(1.0 + jnp.exp(-jnp.abs(z)))
        log_keep = -(jnp.maximum(z, 0.0) + log1p_e)
        on_diagonal = c0 + kb > r0
        if on_diagonal:
            strict = col - row < r0 - c0
            log_keep = jnp.where(strict, log_keep, 0.0)
        hi = log_keep.astype(BF16)
        lo = (log_keep - hi.astype(F32)).astype(BF16)
        sums = (jnp.dot(hi, after, preferred_element_type=F32)
                + jnp.dot(lo, after, preferred_element_type=F32))
        later = sums[:, :kb] + jnp.concatenate([run] * (kb // LANES), axis=1)
        w = jnp.exp((jnp.minimum(z, 0.0) - log1p_e) + later)
        if on_diagonal:
            w = jnp.where(strict, w, 0.0)
        acc = acc + jnp.dot(w.astype(BF16), v_ref[c0:c0 + kb, :], preferred_element_type=F32)
        return acc, run + sums[:, kb:]

    for qi in range(seq // t):
        r0 = qi * t
        q = q_ref[r0:r0 + t, :]
        run = jnp.zeros((t, LANES), F32)
        acc = jnp.zeros((t, HEAD_DIM), F32)
        blocks = list(range((r0 + t) // kb - 1, -1, -1))
        n_near = t // kb + 1
        for j in blocks[:n_near]:
            acc, run = key_block(q, r0, j, acc, run)
        far = blocks[n_near:]
        if far:
            def far_blocks(q=q, r0=r0, far=far, acc=acc, run=run):
                for j in far:
                    acc, run = key_block(q, r0, j, acc, run)
                return acc

            acc = lax.cond(jnp.max(run) >= EXP_UNDERFLOW, far_blocks, lambda acc=acc: acc)
        o_ref[r0:r0 + t, :] = acc.astype(o_ref.dtype)


def _sb_attention(qkv, *, batch, seq, d_model):
    n_heads = d_model // HEAD_DIM
    t = _tile(seq, ATTN_ROWS)
    kb = _tile(t, SB_KEYS)
    blk = lambda first: pl.BlockSpec((seq, HEAD_DIM), lambda b, h: (b, first + h))
    return pl.pallas_call(
        functools.partial(_sb_attn_kernel, seq=seq, t=t, kb=kb),
        out_shape=jax.ShapeDtypeStruct((batch * seq, d_model), BF16),
        grid=(batch, n_heads),
        in_specs=[blk(0), blk(n_heads), blk(2 * n_heads)],
        out_specs=blk(0),
        compiler_params=_params(2),
        name="sb_attn",
    )(qkv, qkv, qkv)


def _pool_kernel(x_ref, halo_ref, w_ref, scale_ref, o_ref, wb_ref, *, tm, seq):
    g = pl.program_id(0)
    i = pl.program_id(1)

    @pl.when(i == 0)
    def _cast():
        wb_ref[...] = w_ref[0, 0].astype(BF16)

    pos0 = (i * tm) % seq
    x = x_ref[...]
    halo = halo_ref[...] * jnp.where(pos0 == 0, 0.0, 1.0).astype(F32)
    xe = jnp.concatenate([halo, x], axis=0)
    pos = pos0 + lax.broadcasted_iota(jnp.int32, (tm, 1), 0)

    for gi, window in enumerate(POOL_WINDOWS):
        @pl.when(g == gi)
        def _window(window=window):
            cur = xe
            span = 1
            while span < window:
                cur = cur + pltpu.roll(cur, span, axis=0)
                span *= 2
            count = jnp.minimum(pos + 1, window).astype(F32)
            pooled = cur[POOL_HALO:, :] / count - x
            y = jnp.dot(pooled.astype(BF16), wb_ref[...], preferred_element_type=F32)
            o_ref[...] = y * scale_ref[0]


def _pool_mixer(h, w_pool, pool_scale, slot, *, seq):
    tokens, d_model = h.shape
    groups, width = w_pool.shape[1], w_pool.shape[2]
    tm = _tile(seq, 512)
    halo_blocks = tm // POOL_HALO
    return pl.pallas_call(
        functools.partial(_pool_kernel, tm=tm, seq=seq),
        out_shape=jax.ShapeDtypeStruct((tokens, d_model), F32),
        grid=(groups, tokens // tm),
        in_specs=[pl.BlockSpec((tm, width), lambda g, i: (i, g)),
                  pl.BlockSpec((POOL_HALO, width), lambda g, i: (jnp.maximum(i * halo_blocks - 1, 0), g)),
                  pl.BlockSpec((1, 1, width, width), lambda g, i: (slot, g, 0, 0)),
                  pl.BlockSpec((1, 1, width), lambda g, i: (slot, 0, g))],
        out_specs=pl.BlockSpec((tm, width), lambda g, i: (i, g)),
        scratch_shapes=[pltpu.VMEM((width, width), BF16)],
        compiler_params=_params(2),
        name="pool_mixer",
    )(h, h, w_pool, pool_scale.reshape(pool_scale.shape[0], 1, d_model))


def _first_argmax(vals):
    best = vals[0]
    for v in vals[1:]:
        best = jnp.maximum(best, v)
    idx = jnp.full(best.shape, len(vals) - 1, jnp.int32)
    for j in range(len(vals) - 2, -1, -1):
        idx = jnp.where(vals[j] == best, j, idx)
    return best, idx


def _route_tile(h, rw_ref, bias_ref, idx_ref, gate_ref, rank_ref, count_ref, run_sc):
    tm = h.shape[0]
    n_experts = rw_ref.shape[0]
    per_group = n_experts // N_GROUPS

    @pl.when(pl.program_id(0) == 0)
    def _init():
        run_sc[...] = jnp.zeros(run_sc.shape, F32)

    logits = lax.dot_general(rw_ref[...], h, (((1,), (1,)), ((), ())),
                             precision=lax.Precision.HIGHEST, preferred_element_type=F32)
    aff = jax.nn.sigmoid(logits)
    sel = aff + bias_ref[...]
    rows = [sel[e:e + 1, :] for e in range(n_experts)]
    aff_rows = [aff[e:e + 1, :] for e in range(n_experts)]

    scores = []
    for g in range(N_GROUPS):
        a, b, c, d = rows[g * per_group:(g + 1) * per_group]
        lo1, hi1 = jnp.minimum(a, b), jnp.maximum(a, b)
        lo2, hi2 = jnp.minimum(c, d), jnp.maximum(c, d)
        scores.append(jnp.maximum(hi1, hi2) + jnp.maximum(jnp.minimum(hi1, hi2), jnp.maximum(lo1, lo2)))
    _, grp = _first_argmax(scores)

    def in_group(table, j):
        out = table[(N_GROUPS - 1) * per_group + j]
        for g in range(N_GROUPS - 2, -1, -1):
            out = jnp.where(grp == g, table[g * per_group + j], out)
        return out

    cand = [in_group(rows, j) for j in range(per_group)]
    cand_aff = [in_group(aff_rows, j) for j in range(per_group)]
    _, loc0 = _first_argmax(cand)
    rest = [jnp.where(loc0 == j, -jnp.inf, cand[j]) for j in range(per_group)]
    _, loc1 = _first_argmax(rest)

    def pick(table, loc):
        out = table[per_group - 1]
        for j in range(per_group - 2, -1, -1):
            out = jnp.where(loc == j, table[j], out)
        return out

    g0, g1 = pick(cand_aff, loc0), pick(cand_aff, loc1)
    total = g0 + g1
    e0 = grp * per_group + loc0
    e1 = grp * per_group + loc1
    idx_ref[0:1, :] = e0
    idx_ref[1:2, :] = e1
    gate_ref[0:1, :] = g0 / total
    gate_ref[1:2, :] = g1 / total

    expert = lax.broadcasted_iota(jnp.int32, (n_experts, tm), 0)
    hit0 = expert == e0
    hit1 = expert == e1
    chosen = jnp.where(hit0, 1.0, jnp.where(hit1, 1.0, 0.0))
    earlier = jnp.where(lax.broadcasted_iota(jnp.int32, (tm, tm), 0)
                        < lax.broadcasted_iota(jnp.int32, (tm, tm), 1), 1.0, 0.0).astype(BF16)
    before = jnp.dot(chosen.astype(BF16), earlier, preferred_element_type=F32) + run_sc[...]
    rank_ref[0:1, :] = jnp.sum(jnp.where(hit0, before, 0.0), axis=0, keepdims=True).astype(jnp.int32)
    rank_ref[1:2, :] = jnp.sum(jnp.where(hit1, before, 0.0), axis=0, keepdims=True).astype(jnp.int32)
    run = run_sc[...] + jnp.sum(chosen, axis=1, keepdims=True)
    run_sc[...] = run
    count_ref[...] = jnp.broadcast_to(run, count_ref.shape).astype(jnp.int32)


def _pack_bf16_pair(lo, hi):
    lo_bits = pltpu.bitcast(lo.astype(BF16).astype(F32), jnp.uint32) >> 16
    hi_bits = pltpu.bitcast(hi.astype(BF16).astype(F32), jnp.uint32) & jnp.uint32(0xFFFF0000)
    return lo_bits | hi_bits


def _unpack_bf16_pair(packed):
    lo = pltpu.bitcast(packed << 16, F32).astype(BF16)
    hi = pltpu.bitcast(packed & jnp.uint32(0xFFFF0000), F32).astype(BF16)
    return lo, hi


def _layer_norm_rows(y, g, b):
    mu = jnp.mean(y, axis=-1, keepdims=True)
    yc = y - mu
    var = jnp.mean(yc * yc, axis=-1, keepdims=True)
    return yc * lax.rsqrt(var + LN_EPS) * g + b


def _ln_route_kernel(h_ref, a_ref, g_ref, b_ref, rw_ref, bias_ref,
                     o_ref, packed_ref, idx_ref, gate_ref, rank_ref, count_ref, run_sc, *, alpha):
    y = alpha * h_ref[...] + a_ref[...]
    out = _layer_norm_rows(y, g_ref[0], b_ref[0])
    o_ref[...] = out
    rows, half = out.shape[0], out.shape[1] // 2
    _slab_store(packed_ref, _pack_bf16_pair(out[:, :half], out[:, half:]), rows, half // LANES)
    _route_tile(out, rw_ref, bias_ref, idx_ref, gate_ref, rank_ref, count_ref, run_sc)


def _ln_route(h, mix, gain, bias, layer, router_w, router_bias, *, alpha):
    tokens, d_model = h.shape
    n_experts = router_w.shape[1]
    depth = gain.shape[0]
    tm = _tile(tokens, 256)
    half_segs = d_model // 2 // LANES
    row = pl.BlockSpec((tm, d_model), lambda i: (i, 0))
    vec = pl.BlockSpec((1, 1, d_model), lambda i: (layer, 0, 0))
    out2 = lambda dt: jax.ShapeDtypeStruct((2, tokens), dt)
    spec2 = pl.BlockSpec((2, tm), lambda i: (0, i))
    return pl.pallas_call(
        functools.partial(_ln_route_kernel, alpha=alpha),
        out_shape=[jax.ShapeDtypeStruct((tokens, d_model), F32),
                   jax.ShapeDtypeStruct((tokens * half_segs, LANES), jnp.uint32),
                   out2(jnp.int32), out2(F32), out2(jnp.int32),
                   jax.ShapeDtypeStruct((n_experts, LANES), jnp.int32)],
        grid=(tokens // tm,),
        in_specs=[row, row, vec, vec,
                  pl.BlockSpec((n_experts, d_model), lambda i: (0, 0)),
                  pl.BlockSpec((n_experts, 1), lambda i: (0, 0))],
        out_specs=[row, pl.BlockSpec((tm * half_segs, LANES), lambda i: (i, 0)),
                   spec2, spec2, spec2, pl.BlockSpec((n_experts, LANES), lambda i: (0, 0))],
        scratch_shapes=[pltpu.VMEM((n_experts, 1), F32)],
        compiler_params=_params(1), name="ln_route",
    )(h, mix, gain.reshape(depth, 1, d_model), bias.reshape(depth, 1, d_model),
      router_w.T, router_bias.reshape(n_experts, 1).astype(F32))


def _ln_moe_kernel(slot_ref, h_ref, ys_ref, gate_ref, g_ref, b_ref, *rest, alpha, tokens):
    *o_refs, ybuf, sems = rest
    rows, d_model = h_ref.shape
    segs = d_model // LANES
    i = pl.program_id(0)
    n_steps = pl.num_programs(0)

    def gather(step, buf):
        def body(r, carry):
            for k in range(2):
                src = pl.multiple_of(slot_ref[k * tokens + step * rows + r] * segs, segs)
                dst = pl.multiple_of(r * segs, segs)
                pltpu.make_async_copy(ys_ref.at[pl.ds(src, segs)], ybuf.at[buf, k, pl.ds(dst, segs)],
                                      sems.at[buf]).start()
            return carry
        lax.fori_loop(0, rows, body, 0, unroll=4)

    @pl.when(i == 0)
    def _first():
        gather(0, 0)

    @pl.when(i + 1 < n_steps)
    def _ahead():
        gather(i + 1, (i + 1) % 2)

    buf = i % 2
    for k in range(2):
        pltpu.make_async_copy(ys_ref.at[pl.ds(0, rows * segs)], ybuf.at[buf, k], sems.at[buf]).wait()

    gates = gate_ref[...]
    ffn = (gates[:, 0:1] * _slab_load(ybuf, rows, segs, lead=(buf, 0))
           + gates[:, 1:2] * _slab_load(ybuf, rows, segs, lead=(buf, 1)))
    y = alpha * h_ref[...] + ffn
    out = _layer_norm_rows(y, g_ref[0], b_ref[0])
    o_refs[0][...] = out
    if len(o_refs) > 1:
        o_refs[1][...] = out.astype(BF16)


def _ln_moe(h, ys, slot, gates, gain, bias, layer, *, alpha, emit_bf16):
    tokens, d_model = h.shape
    tm = _tile(tokens, 256)
    segs = d_model // LANES
    depth = gain.shape[0]
    row = pl.BlockSpec((tm, d_model), lambda i, s: (i, 0))
    vec = pl.BlockSpec((1, 1, d_model), lambda i, s: (layer, 0, 0))
    shapes = [jax.ShapeDtypeStruct((tokens, d_model), F32)]
    specs = [row]
    if emit_bf16:
        shapes.append(jax.ShapeDtypeStruct((tokens, d_model), BF16))
        specs.append(row)
    return pl.pallas_call(
        functools.partial(_ln_moe_kernel, alpha=alpha, tokens=tokens),
        out_shape=shapes,
        grid_spec=pltpu.PrefetchScalarGridSpec(
            num_scalar_prefetch=1, grid=(tokens // tm,),
            in_specs=[row, pl.BlockSpec(memory_space=pl.ANY),
                      pl.BlockSpec((tm, 2), lambda i, s: (i, 0)), vec, vec],
            out_specs=specs,
            scratch_shapes=[pltpu.VMEM((2, 2, tm * segs, LANES), F32),
                            pltpu.SemaphoreType.DMA((2,))]),
        compiler_params=_params(1), name="ln_moe",
    )(slot, h, ys, gates, gain.reshape(depth, 1, d_model), bias.reshape(depth, 1, d_model))


def _dispatch_kernel(slot_ref, zero_row_ref, zero_on_ref, x_ref, dst_ref, zero_sc, sems,
                     *, tokens, rows, n_zero, segs):
    @pl.when(pl.program_id(0) == 0)
    def _zero_fill():
        zero_sc[...] = jnp.zeros(zero_sc.shape, zero_sc.dtype)
        span = zero_sc.shape[0]

        def zero_copy(e):
            start = pl.multiple_of(zero_row_ref[e] * segs, span)
            return pltpu.make_async_copy(zero_sc, dst_ref.at[pl.ds(start, span)], sems.at[0])

        for e in range(n_zero):
            @pl.when(zero_on_ref[e] > 0)
            def _start(e=e):
                zero_copy(e).start()
        for e in range(n_zero):
            @pl.when(zero_on_ref[e] > 0)
            def _wait(e=e):
                zero_copy(e).wait()

    base = pl.program_id(0) * rows

    def body(r, carry):
        src = pl.multiple_of(r * segs, segs)
        for k in range(2):
            dst = pl.multiple_of(slot_ref[k * tokens + base + r] * segs, segs)
            pltpu.make_async_copy(x_ref.at[pl.ds(src, segs)], dst_ref.at[pl.ds(dst, segs)],
                                  sems.at[1]).start()
        return carry

    lax.fori_loop(0, rows, body, 0, unroll=4)
    for k in range(2):
        pltpu.make_async_copy(x_ref, dst_ref.at[pl.ds(0, rows * segs)], sems.at[1]).wait()


def _dispatch(hp, slot, zero_rows, zero_on, *, tokens, n_slots, segs, tm):
    rows = _tile(tokens, 256)
    kern = functools.partial(_dispatch_kernel, tokens=tokens, rows=rows, n_zero=zero_rows.shape[0], segs=segs)
    return pl.pallas_call(
        kern,
        out_shape=jax.ShapeDtypeStruct((n_slots * segs, LANES), hp.dtype),
        grid_spec=pltpu.PrefetchScalarGridSpec(
            num_scalar_prefetch=3, grid=(tokens // rows,),
            in_specs=[pl.BlockSpec((rows * segs, LANES), lambda i, *_: (i, 0))],
            out_specs=pl.BlockSpec(memory_space=pl.ANY),
            scratch_shapes=[pltpu.VMEM((tm * segs, LANES), hp.dtype),
                            pltpu.SemaphoreType.DMA((2,))]),
        compiler_params=_params(1),
        name="dispatch",
    )(slot, zero_rows, zero_on, hp)


def _cast_kernel(x_ref, o_ref):
    o_ref[...] = x_ref[...].astype(o_ref.dtype)


def _to_bf16(w):
    n, r, c = w.shape
    tr = _tile(r, 1024)
    return pl.pallas_call(
        _cast_kernel,
        out_shape=jax.ShapeDtypeStruct(w.shape, BF16),
        grid=(n, r // tr),
        in_specs=[pl.BlockSpec((1, tr, c), lambda a, b: (a, b, 0))],
        out_specs=pl.BlockSpec((1, tr, c), lambda a, b: (a, b, 0)),
        compiler_params=_params(2),
        name="to_bf16",
    )(w)


def _expert_kernel(tile_expert_ref, tile_block_ref, n_used_ref, x_ref, wg_ref, wu_ref, wd_ref, o_ref,
                   *, d_model, tm):
    del tile_expert_ref, tile_block_ref
    half = d_model // 2

    @pl.when(pl.program_id(0) >= n_used_ref[0])
    def _unused_tile():
        o_ref[...] = jnp.zeros(o_ref.shape, o_ref.dtype)

    @pl.when(pl.program_id(0) < n_used_ref[0])
    def _compute():
        lo, hi = _unpack_bf16_pair(_slab_load(x_ref, tm, half // LANES))
        gate = (jnp.dot(lo, wg_ref[0, :half, :], preferred_element_type=F32)
                + jnp.dot(hi, wg_ref[0, half:, :], preferred_element_type=F32))
        up = (jnp.dot(lo, wu_ref[0, :half, :], preferred_element_type=F32)
              + jnp.dot(hi, wu_ref[0, half:, :], preferred_element_type=F32))
        act = (gate * jax.nn.sigmoid(gate) * up).astype(BF16)
        segs = d_model // LANES
        col = _tile(d_model, 1024)
        for c in range(d_model // col):
            y = jnp.dot(act, wd_ref[0, :, c * col:(c + 1) * col], preferred_element_type=F32)
            for s in range(col // LANES):
                o_ref[pl.ds(c * (col // LANES) + s, tm, stride=segs), :] = y[:, s * LANES:(s + 1) * LANES]


def _expert_mlp(xs, wg, wu, wd, tile_expert, tile_block, n_used, *, layer, n_experts, tm, n_tiles):
    d_model, d_ff = wg.shape[1], wg.shape[2]
    half_segs = d_model // 2 // LANES
    segs = d_model // LANES
    base = layer * n_experts
    return pl.pallas_call(
        functools.partial(_expert_kernel, d_model=d_model, tm=tm),
        out_shape=jax.ShapeDtypeStruct((n_tiles * tm * segs, LANES), F32),
        grid_spec=pltpu.PrefetchScalarGridSpec(
            num_scalar_prefetch=3, grid=(n_tiles,),
            in_specs=[pl.BlockSpec((tm * half_segs, LANES), lambda i, te, tb, nu: (tb[i], 0)),
                      pl.BlockSpec((1, d_model, d_ff), lambda i, te, tb, nu: (base + te[i], 0, 0)),
                      pl.BlockSpec((1, d_model, d_ff), lambda i, te, tb, nu: (base + te[i], 0, 0)),
                      pl.BlockSpec((1, d_ff, d_model), lambda i, te, tb, nu: (base + te[i], 0, 0))],
            out_specs=pl.BlockSpec((tm * segs, LANES), lambda i, te, tb, nu: (i, 0))),
        compiler_params=_params(1),
        name="expert_mlp",
    )(tile_expert, tile_block, n_used, xs, wg, wu, wd)


def _grouped_moe(hp, routing, wg, wu, wd, *, tokens, layer, tm):
    idx, gates, rank, counts = routing
    n_experts = counts.shape[0]
    d_model = wg.shape[1]
    half_segs = d_model // 2 // LANES

    counts = counts[:, 0]
    padded = ((counts + tm - 1) // tm) * tm
    seg_end = jnp.cumsum(padded)
    seg_start = seg_end - padded
    n_tiles = (2 * tokens) // tm + n_experts
    slot = (jnp.take(seg_start, idx.reshape(-1)) + rank.reshape(-1)).astype(jnp.int32)
    n_used = (seg_end[-1] // tm).astype(jnp.int32)
    tile_block = jnp.minimum(jnp.arange(n_tiles, dtype=jnp.int32), n_used - 1)
    tile_expert = jnp.sum(seg_end[None, :] <= (tile_block * tm)[:, None], axis=1).astype(jnp.int32)
    tile_expert = jnp.minimum(tile_expert, n_experts - 1)
    spare = n_used + jnp.arange(n_experts, dtype=jnp.int32)
    zero_rows = jnp.concatenate([jnp.maximum(seg_end - tm, 0).astype(jnp.int32),
                                 jnp.minimum(spare, n_tiles - 1) * tm])
    zero_on = jnp.concatenate([padded > counts, spare < n_tiles]).astype(jnp.int32)

    xs = _dispatch(hp, slot, zero_rows, zero_on, tokens=tokens, n_slots=n_tiles * tm,
                   segs=half_segs, tm=tm)
    ys = _expert_mlp(xs, wg, wu, wd, tile_expert, tile_block, n_used.reshape(1),
                     layer=layer, n_experts=n_experts, tm=tm, n_tiles=n_tiles)
    return ys, slot, gates.T


def kernel(x, w_qkv_diff, w_o_diff, lambda_q1, lambda_k1, lambda_q2, lambda_k2, subln_g, w_pool,
           pool_scale, w_qkv_sb, w_o_sb, ln_mix_g, ln_mix_b, ln_ffn_g, ln_ffn_b, router_w,
           router_bias, w_gate, w_up, w_down):
    batch, seq, d_model = x.shape
    depth = ln_mix_g.shape[0]
    n_experts = router_w.shape[1]
    d_ff = w_gate.shape[-1]
    tokens = batch * seq
    alpha = (2.0 * depth) ** 0.25
    moe_tile = _tile(2 * tokens, 256)

    wg = _to_bf16(w_gate.reshape(depth * n_experts, d_model, d_ff))
    wu = _to_bf16(w_up.reshape(depth * n_experts, d_model, d_ff))
    wd = _to_bf16(w_down.reshape(depth * n_experts, d_ff, d_model))
    cos, sin = _rope_tables(seq)

    h = x.reshape(tokens, d_model)
    hb = None
    for i in range(depth):
        kind, slot = i % N_MIXERS, i // N_MIXERS
        if kind == 1:
            mix = _pool_mixer(h, w_pool, pool_scale, slot, seq=seq)
        else:
            if hb is None:
                hb = h.astype(BF16)
            if kind == 0:
                lambda_init = 0.8 - 0.6 * math.exp(-0.3 * i)
                qkv = _qkv_project(hb, w_qkv_diff, slot, cos, sin, seq=seq, rope=True)
                o = _diff_attention(qkv, lambda_q1[slot], lambda_k1[slot], lambda_q2[slot],
                                    lambda_k2[slot], subln_g[slot], batch=batch, seq=seq,
                                    d_model=d_model, lambda_init=lambda_init)
                mix = _project(o, w_o_diff, slot, F32)
            else:
                qkv = _qkv_project(hb, w_qkv_sb, slot, cos, sin, seq=seq, rope=False)
                o = _sb_attention(qkv, batch=batch, seq=seq, d_model=d_model)
                mix = _project(o, w_o_sb, slot, F32)
        h, hp, *routing = _ln_route(h, mix, ln_mix_g, ln_mix_b, i, router_w, router_bias, alpha=alpha)
        ys, slot_of, gates = _grouped_moe(hp, routing, wg, wu, wd, tokens=tokens, layer=i, tm=moe_tile)
        emit_bf16 = i + 1 < depth and (i + 1) % N_MIXERS != 1
        outs = _ln_moe(h, ys, slot_of, gates, ln_ffn_g, ln_ffn_b, i, alpha=alpha, emit_bf16=emit_bf16)
        h = outs[0]
        hb = outs[1] if emit_bf16 else None
    return h.reshape(batch, seq, d_model)
```

```python
import functools
import math

import jax
import jax.numpy as jnp
from jax import lax
from jax.experimental import pallas as pl
from jax.experimental.pallas import tpu as pltpu

F32 = jnp.float32
BF16 = jnp.bfloat16

HEAD_DIM = 128
CHUNK = 64
N_MIXERS = 3
POOL_WINDOWS = (2, 4, 8, 16)
POOL_HALO = 16
N_GROUPS = 4
ROPE_THETA = 10000.0
LN_EPS = 1e-5
RMS_EPS = 1e-5

LANES = 128
VMEM_LIMIT = 60 * 1024 * 1024
ATTN_ROWS = 512
SB_KEYS = 256
EXP_UNDERFLOW = -104.0


def _tile(n, pref):
    return pref if n % pref == 0 else n


def _params(n_axes, vmem=VMEM_LIMIT):
    return pltpu.CompilerParams(dimension_semantics=("arbitrary",) * n_axes, vmem_limit_bytes=vmem)


def _proj_kernel(x_ref, w_ref, o_ref, wb_ref):
    @pl.when(pl.program_id(1) == 0)
    def _cast():
        wb_ref[...] = w_ref[0].astype(BF16)

    o_ref[...] = jnp.dot(x_ref[...], wb_ref[...], preferred_element_type=F32).astype(o_ref.dtype)


def _project(x, w, slot, out_dtype):
    m, k = x.shape
    n = w.shape[2]
    tm, tn = _tile(m, 1024), _tile(n, 512)
    return pl.pallas_call(
        _proj_kernel,
        out_shape=jax.ShapeDtypeStruct((m, n), out_dtype),
        grid=(n // tn, m // tm),
        in_specs=[pl.BlockSpec((tm, k), lambda j, i: (i, 0)),
                  pl.BlockSpec((1, k, tn), lambda j, i: (slot, 0, j))],
        out_specs=pl.BlockSpec((tm, tn), lambda j, i: (i, j)),
        scratch_shapes=[pltpu.VMEM((k, tn), BF16)],
        compiler_params=_params(2),
        name="proj",
    )(x, w)


def _qkv_kernel(x_ref, w_ref, cos_ref, sin_ref, o_ref, wb_ref, *, d_model, tn, rope, scale):
    col0 = pl.program_id(0) * tn

    @pl.when(pl.program_id(1) == 0)
    def _cast():
        wb_ref[...] = w_ref[0].astype(BF16)

    acc = jnp.dot(x_ref[...], wb_ref[...], preferred_element_type=F32)
    q_scale = jnp.where(col0 < d_model, scale, 1.0).astype(F32)
    if rope:
        @pl.when(col0 < 2 * d_model)
        def _rotary():
            cos, sin = cos_ref[...], sin_ref[...]
            for c in range(tn // HEAD_DIM):
                t = acc[:, c * HEAD_DIM:(c + 1) * HEAD_DIM]
                r = (t * cos + pltpu.roll(t, HEAD_DIM // 2, axis=1) * sin) * q_scale
                o_ref[:, c * HEAD_DIM:(c + 1) * HEAD_DIM] = r.astype(o_ref.dtype)

        @pl.when(col0 >= 2 * d_model)
        def _plain():
            o_ref[...] = acc.astype(o_ref.dtype)
    else:
        o_ref[...] = (acc * q_scale).astype(o_ref.dtype)


def _qkv_project(x, w, slot, cos, sin, *, seq, rope):
    m, k = x.shape
    n = w.shape[2]
    tm, tn = _tile(seq, 1024), _tile(k, 512)
    n_pos = seq // tm
    kern = functools.partial(_qkv_kernel, d_model=k, tn=tn, rope=rope, scale=HEAD_DIM ** -0.5)
    return pl.pallas_call(
        kern,
        out_shape=jax.ShapeDtypeStruct((m, n), BF16),
        grid=(n // tn, m // tm),
        in_specs=[pl.BlockSpec((tm, k), lambda j, i: (i, 0)),
                  pl.BlockSpec((1, k, tn), lambda j, i: (slot, 0, j)),
                  pl.BlockSpec((tm, HEAD_DIM), lambda j, i: (i % n_pos, 0)),
                  pl.BlockSpec((tm, HEAD_DIM), lambda j, i: (i % n_pos, 0))],
        out_specs=pl.BlockSpec((tm, tn), lambda j, i: (i, j)),
        scratch_shapes=[pltpu.VMEM((k, tn), BF16)],
        compiler_params=_params(2),
        name="qkv_proj",
    )(x, w, cos, sin)


def _rope_tables(seq):
    half = HEAD_DIM // 2
    inv_freq = ROPE_THETA ** (-jnp.arange(half, dtype=F32) / half)
    ang = jnp.arange(seq).astype(F32)[:, None] * inv_freq[None, :]
    cos, sin = jnp.cos(ang), jnp.sin(ang)
    return jnp.concatenate([cos, cos], axis=-1), jnp.concatenate([-sin, sin], axis=-1)


def _nt_dot(a, b):
    return lax.dot_general(a, b, (((1,), (1,)), ((), ())), preferred_element_type=F32)


def _diff_attn_kernel(q1_ref, q2_ref, k1_ref, k2_ref, v_ref, lq1_ref, lk1_ref, lq2_ref, lk2_ref,
                      g_ref, o_ref, *, seq, t, lambda_init):
    lam = (jnp.exp(jnp.sum(lq1_ref[...] * lk1_ref[...], axis=-1, keepdims=True))
           - jnp.exp(jnp.sum(lq2_ref[...] * lk2_ref[...], axis=-1, keepdims=True)) + lambda_init)
    row_chunk = lax.broadcasted_iota(jnp.int32, (t, t), 0) // CHUNK
    col_chunk = lax.broadcasted_iota(jnp.int32, (t, t), 1) // CHUNK
    visible = col_chunk <= row_chunk

    for qi in range(seq // t):
        r0 = qi * t
        heads = []
        for q_ref, k_ref in ((q1_ref, k1_ref), (q2_ref, k2_ref)):
            q = q_ref[r0:r0 + t, :]
            s_diag = jnp.where(visible, _nt_dot(q, k_ref[r0:r0 + t, :]), -jnp.inf)
            m = jnp.max(s_diag, axis=-1, keepdims=True)
            if qi:
                s_past = _nt_dot(q, k_ref[0:r0, :])
                m = jnp.maximum(m, jnp.max(s_past, axis=-1, keepdims=True))
            p = jnp.exp(s_diag - m)
            denom = jnp.sum(p, axis=-1, keepdims=True)
            acc = jnp.dot(p.astype(BF16), v_ref[r0:r0 + t, :], preferred_element_type=F32)
            if qi:
                p = jnp.exp(s_past - m)
                denom = denom + jnp.sum(p, axis=-1, keepdims=True)
                acc = acc + jnp.dot(p.astype(BF16), v_ref[0:r0, :], preferred_element_type=F32)
            heads.append(acc / denom)
        o = heads[0] - lam * heads[1]
        o = o * lax.rsqrt(jnp.mean(o * o, axis=-1, keepdims=True) + RMS_EPS) * g_ref[...]
        o_ref[r0:r0 + t, :] = (o * (1.0 - lambda_init)).astype(o_ref.dtype)


def _diff_attention(qkv, lq1, lk1, lq2, lk2, subln_g, *, batch, seq, d_model, lambda_init):
    n_heads = d_model // (2 * HEAD_DIM)
    t = _tile(seq, ATTN_ROWS)
    kq = d_model // HEAD_DIM
    kv = 2 * d_model // (2 * HEAD_DIM)
    vec = lambda a: a.reshape(1, -1).astype(F32)
    small = lambda n: pl.BlockSpec((1, n), lambda b, h: (0, 0))
    kern = functools.partial(_diff_attn_kernel, seq=seq, t=t, lambda_init=lambda_init)
    return pl.pallas_call(
        kern,
        out_shape=jax.ShapeDtypeStruct((batch * seq, d_model), BF16),
        grid=(batch, n_heads),
        in_specs=[pl.BlockSpec((seq, HEAD_DIM), lambda b, h: (b, 2 * h)),
                  pl.BlockSpec((seq, HEAD_DIM), lambda b, h: (b, 2 * h + 1)),
                  pl.BlockSpec((seq, HEAD_DIM), lambda b, h: (b, kq + 2 * h)),
                  pl.BlockSpec((seq, HEAD_DIM), lambda b, h: (b, kq + 2 * h + 1)),
                  pl.BlockSpec((seq, 2 * HEAD_DIM), lambda b, h: (b, kv + h)),
                  small(HEAD_DIM), small(HEAD_DIM), small(HEAD_DIM), small(HEAD_DIM),
                  small(2 * HEAD_DIM)],
        out_specs=pl.BlockSpec((seq, 2 * HEAD_DIM), lambda b, h: (b, h)),
        compiler_params=_params(2),
        name="diff_attn",
    )(qkv, qkv, qkv, qkv, qkv, vec(lq1), vec(lk1), vec(lq2), vec(lk2), vec(subln_g))


def _sb_attn_kernel(q_ref, k_ref, v_ref, o_ref, *, seq, t, kb):
    src = lax.broadcasted_iota(jnp.int32, (kb, kb + LANES), 0)
    dst = lax.broadcasted_iota(jnp.int32, (kb, kb + LANES), 1)
    after = jnp.where((src > dst) | (dst >= kb), 1.0, 0.0).astype(BF16)
    row = lax.broadcasted_iota(jnp.int32, (t, kb), 0)
    col = lax.broadcasted_iota(jnp.int32, (t, kb), 1)

    def key_block(q, r0, j, acc, run):
        c0 = j * kb
        z = _nt_dot(q, k_ref[c0:c0 + kb, :])
        log1p_e = jnp.log(1.0 + jnp.exp(-jnp.abs(z)))
        log_keep = -(jnp.maximum(z, 0.0) + log1p_e)
        on_diagonal = c0 + kb > r0
        if on_diagonal:
            strict = col - row < r0 - c0
            log_keep = jnp.where(strict, log_keep, 0.0)
        hi = log_keep.astype(BF16)
        lo = (log_keep - hi.astype(F32)).astype(BF16)
        sums = (jnp.dot(hi, after, preferred_element_type=F32)
                + jnp.dot(lo, after, preferred_element_type=F32))
        later = sums[:, :kb] + jnp.concatenate([run] * (kb // LANES), axis=1)
        w = jnp.exp((jnp.minimum(z, 0.0) - log1p_e) + later)
        if on_diagonal:
            w = jnp.where(strict, w, 0.0)
        acc = acc + jnp.dot(w.astype(BF16), v_ref[c0:c0 + kb, :], preferred_element_type=F32)
        return acc, run + sums[:, kb:]

    for qi in range(seq // t):
        r0 = qi * t
        q = q_ref[r0:r0 + t, :]
        run = jnp.zeros((t, LANES), F32)
        acc = jnp.zeros((t, HEAD_DIM), F32)
        blocks = list(range((r0 + t) // kb - 1, -1, -1))
        n_near = t // kb + 1
        for j in blocks[:n_near]:
            acc, run = key_block(q, r0, j, acc, run)
        far = blocks[n_near:]
        if far:
            def far_blocks(q=q, r0=r0, far=far, acc=acc, run=run):
                for j in far:
                    acc, run = key_block(q, r0, j, acc, run)
                return acc

            acc = lax.cond(jnp.max(run) >= EXP_UNDERFLOW, far_blocks, lambda acc=acc: acc)
        o_ref[r0:r0 + t, :] = acc.astype(o_ref.dtype)


def _sb_attention(qkv, *, batch, seq, d_model):
    n_heads = d_model // HEAD_DIM
    t = _tile(seq, ATTN_ROWS)
    kb = _tile(t, SB_KEYS)
    blk = lambda first: pl.BlockSpec((seq, HEAD_DIM), lambda b, h: (b, first + h))
    return pl.pallas_call(
        functools.partial(_sb_attn_kernel, seq=seq, t=t, kb=kb),
        out_shape=jax.ShapeDtypeStruct((batch * seq, d_model), BF16),
        grid=(batch, n_heads),
        in_specs=[blk(0), blk(n_heads), blk(2 * n_heads)],
        out_specs=blk(0),
        compiler_params=_params(2),
        name="sb_attn",
    )(qkv, qkv, qkv)


def _pool_kernel(x_ref, halo_ref, w_ref, scale_ref, o_ref, wb_ref, *, tm, seq):
    g = pl.program_id(0)
    i = pl.program_id(1)

    @pl.when(i == 0)
    def _cast():
        wb_ref[...] = w_ref[0, 0].astype(BF16)

    pos0 = (i * tm) % seq
    x = x_ref[...]
    halo = halo_ref[...] * jnp.where(pos0 == 0, 0.0, 1.0).astype(F32)
    xe = jnp.concatenate([halo, x], axis=0)
    pos = pos0 + lax.broadcasted_iota(jnp.int32, (tm, 1), 0)

    for gi, window in enumerate(POOL_WINDOWS):
        @pl.when(g == gi)
        def _window(window=window):
            cur = xe
            span = 1
            while span < window:
                cur = cur + pltpu.roll(cur, span, axis=0)
                span *= 2
            count = jnp.minimum(pos + 1, window).astype(F32)
            pooled = cur[POOL_HALO:, :] / count - x
            y = jnp.dot(pooled.astype(BF16), wb_ref[...], preferred_element_type=F32)
            o_ref[...] = y * scale_ref[0]


def _pool_mixer(h, w_pool, pool_scale, slot, *, seq):
    tokens, d_model = h.shape
    groups, width = w_pool.shape[1], w_pool.shape[2]
    tm = _tile(seq, 512)
    halo_blocks = tm // POOL_HALO
    return pl.pallas_call(
        functools.partial(_pool_kernel, tm=tm, seq=seq),
        out_shape=jax.ShapeDtypeStruct((tokens, d_model), F32),
        grid=(groups, tokens // tm),
        in_specs=[pl.BlockSpec((tm, width), lambda g, i: (i, g)),
                  pl.BlockSpec((POOL_HALO, width), lambda g, i: (jnp.maximum(i * halo_blocks - 1, 0), g)),
                  pl.BlockSpec((1, 1, width, width), lambda g, i: (slot, g, 0, 0)),
                  pl.BlockSpec((1, 1, width), lambda g, i: (slot, 0, g))],
        out_specs=pl.BlockSpec((tm, width), lambda g, i: (i, g)),
        scratch_shapes=[pltpu.VMEM((width, width), BF16)],
        compiler_params=_params(2),
        name="pool_mixer",
    )(h, h, w_pool, pool_scale.reshape(pool_scale.shape[0], 1, d_model))


def _first_argmax(vals):
    best = vals[0]
    for v in vals[1:]:
        best = jnp.maximum(best, v)
    idx = jnp.full(best.shape, len(vals) - 1, jnp.int32)
    for j in range(len(vals) - 2, -1, -1):
        idx = jnp.where(vals[j] == best, j, idx)
    return best, idx


def _route_tile(h, rw_ref, bias_ref, idx_ref, gate_ref, rank_ref, count_ref, run_sc):
    tm = h.shape[0]
    n_experts = rw_ref.shape[0]
    per_group = n_experts // N_GROUPS

    @pl.when(pl.program_id(0) == 0)
    def _init():
        run_sc[...] = jnp.zeros(run_sc.shape, F32)

    logits = lax.dot_general(rw_ref[...], h, (((1,), (1,)), ((), ())),
                             precision=lax.Precision.HIGHEST, preferred_element_type=F32)
    aff = jax.nn.sigmoid(logits)
    sel = aff + bias_ref[...]
    rows = [sel[e:e + 1, :] for e in range(n_experts)]
    aff_rows = [aff[e:e + 1, :] for e in range(n_experts)]

    scores = []
    for g in range(N_GROUPS):
        a, b, c, d = rows[g * per_group:(g + 1) * per_group]
        lo1, hi1 = jnp.minimum(a, b), jnp.maximum(a, b)
        lo2, hi2 = jnp.minimum(c, d), jnp.maximum(c, d)
        scores.append(jnp.maximum(hi1, hi2) + jnp.maximum(jnp.minimum(hi1, hi2), jnp.maximum(lo1, lo2)))
    _, grp = _first_argmax(scores)

    def in_group(table, j):
        out = table[(N_GROUPS - 1) * per_group + j]
        for g in range(N_GROUPS - 2, -1, -1):
            out = jnp.where(grp == g, table[g * per_group + j], out)
        return out

    cand = [in_group(rows, j) for j in range(per_group)]
    cand_aff = [in_group(aff_rows, j) for j in range(per_group)]
    _, loc0 = _first_argmax(cand)
    rest = [jnp.where(loc0 == j, -jnp.inf, cand[j]) for j in range(per_group)]
    _, loc1 = _first_argmax(rest)

    def pick(table, loc):
        out = table[per_group - 1]
        for j in range(per_group - 2, -1, -1):
            out = jnp.where(loc == j, table[j], out)
        return out

    g0, g1 = pick(cand_aff, loc0), pick(cand_aff, loc1)
    total = g0 + g1
    e0 = grp * per_group + loc0
    e1 = grp * per_group + loc1
    idx_ref[0:1, :] = e0
    idx_ref[1:2, :] = e1
    gate_ref[0:1, :] = g0 / total
    gate_ref[1:2, :] = g1 / total

    expert = lax.broadcasted_iota(jnp.int32, (n_experts, tm), 0)
    hit0 = expert == e0
    hit1 = expert == e1
    chosen = jnp.where(hit0, 1.0, jnp.where(hit1, 1.0, 0.0))
    earlier = jnp.where(lax.broadcasted_iota(jnp.int32, (tm, tm), 0)
                        < lax.broadcasted_iota(jnp.int32, (tm, tm), 1), 1.0, 0.0).astype(BF16)
    before = jnp.dot(chosen.astype(BF16), earlier, preferred_element_type=F32) + run_sc[...]
    rank_ref[0:1, :] = jnp.sum(jnp.where(hit0, before, 0.0), axis=0, keepdims=True).astype(jnp.int32)
    rank_ref[1:2, :] = jnp.sum(jnp.where(hit1, before, 0.0), axis=0, keepdims=True).astype(jnp.int32)
    run = run_sc[...] + jnp.sum(chosen, axis=1, keepdims=True)
    run_sc[...] = run
    count_ref[...] = jnp.broadcast_to(run, count_ref.shape).astype(jnp.int32)


def _pack_bf16_pair(lo, hi):
    lo_bits = pltpu.bitcast(lo.astype(BF16).astype(F32), jnp.uint32) >> 16
    hi_bits = pltpu.bitcast(hi.astype(BF16).astype(F32), jnp.uint32) & jnp.uint32(0xFFFF0000)
    return lo_bits | hi_bits


def _unpack_bf16_pair(packed):
    lo = pltpu.bitcast(packed << 16, F32).astype(BF16)
    hi = pltpu.bitcast(packed & jnp.uint32(0xFFFF0000), F32).astype(BF16)
    return lo, hi


def _layer_norm_rows(y, g, b):
    mu = jnp.mean(y, axis=-1, keepdims=True)
    yc = y - mu
    var = jnp.mean(yc * yc, axis=-1, keepdims=True)
    return yc * lax.rsqrt(var + LN_EPS) * g + b


def _ln_route_kernel(h_ref, a_ref, g_ref, b_ref, rw_ref, bias_ref,
                     o_ref, packed_ref, idx_ref, gate_ref, rank_ref, count_ref, run_sc, *, alpha):
    y = alpha * h_ref[...] + a_ref[...]
    out = _layer_norm_rows(y, g_ref[0], b_ref[0])
    o_ref[...] = out
    half = out.shape[1] // 2
    packed_ref[...] = _pack_bf16_pair(out[:, :half], out[:, half:])
    _route_tile(out, rw_ref, bias_ref, idx_ref, gate_ref, rank_ref, count_ref, run_sc)


def _ln_route(h, mix, gain, bias, layer, router_w, router_bias, *, alpha):
    tokens, d_model = h.shape
    n_experts = router_w.shape[1]
    depth = gain.shape[0]
    tm = _tile(tokens, 256)
    half = d_model // 2
    row = pl.BlockSpec((tm, d_model), lambda i: (i, 0))
    vec = pl.BlockSpec((1, 1, d_model), lambda i: (layer, 0, 0))
    out2 = lambda dt: jax.ShapeDtypeStruct((2, tokens), dt)
    spec2 = pl.BlockSpec((2, tm), lambda i: (0, i))
    return pl.pallas_call(
        functools.partial(_ln_route_kernel, alpha=alpha),
        out_shape=[jax.ShapeDtypeStruct((tokens, d_model), F32),
                   jax.ShapeDtypeStruct((tokens, half), jnp.uint32),
                   out2(jnp.int32), out2(F32), out2(jnp.int32),
                   jax.ShapeDtypeStruct((n_experts, LANES), jnp.int32)],
        grid=(tokens // tm,),
        in_specs=[row, row, vec, vec,
                  pl.BlockSpec((n_experts, d_model), lambda i: (0, 0)),
                  pl.BlockSpec((n_experts, 1), lambda i: (0, 0))],
        out_specs=[row, pl.BlockSpec((tm, half), lambda i: (i, 0)),
                   spec2, spec2, spec2, pl.BlockSpec((n_experts, LANES), lambda i: (0, 0))],
        scratch_shapes=[pltpu.VMEM((n_experts, 1), F32)],
        compiler_params=_params(1), name="ln_route",
    )(h, mix, gain.reshape(depth, 1, d_model), bias.reshape(depth, 1, d_model),
      router_w.T, router_bias.reshape(n_experts, 1).astype(F32))


def _ln_moe_kernel(slot_ref, h_ref, ys_ref, gate_ref, g_ref, b_ref, *rest, alpha, tokens):
    *o_refs, ybuf, sems = rest
    rows, d_model = h_ref.shape
    i = pl.program_id(0)
    n_steps = pl.num_programs(0)

    def gather(step, buf):
        def body(r, carry):
            for k in range(2):
                src = slot_ref[k * tokens + step * rows + r]
                pltpu.make_async_copy(ys_ref.at[pl.ds(src, 1)], ybuf.at[buf, k, pl.ds(r, 1)],
                                      sems.at[buf]).start()
            return carry
        lax.fori_loop(0, rows, body, 0, unroll=4)

    @pl.when(i == 0)
    def _first():
        gather(0, 0)

    @pl.when(i + 1 < n_steps)
    def _ahead():
        gather(i + 1, (i + 1) % 2)

    buf = i % 2
    for k in range(2):
        pltpu.make_async_copy(ys_ref.at[pl.ds(0, rows)], ybuf.at[buf, k], sems.at[buf]).wait()

    gates = gate_ref[...]
    ffn = gates[:, 0:1] * ybuf[buf, 0] + gates[:, 1:2] * ybuf[buf, 1]
    y = alpha * h_ref[...] + ffn
    out = _layer_norm_rows(y, g_ref[0], b_ref[0])
    o_refs[0][...] = out
    if len(o_refs) > 1:
        o_refs[1][...] = out.astype(BF16)


def _ln_moe(h, ys, slot, gates, gain, bias, layer, *, alpha, emit_bf16):
    tokens, d_model = h.shape
    tm = _tile(tokens, 256)
    depth = gain.shape[0]
    row = pl.BlockSpec((tm, d_model), lambda i, s: (i, 0))
    vec = pl.BlockSpec((1, 1, d_model), lambda i, s: (layer, 0, 0))
    shapes = [jax.ShapeDtypeStruct((tokens, d_model), F32)]
    specs = [row]
    if emit_bf16:
        shapes.append(jax.ShapeDtypeStruct((tokens, d_model), BF16))
        specs.append(row)
    return pl.pallas_call(
        functools.partial(_ln_moe_kernel, alpha=alpha, tokens=tokens),
        out_shape=shapes,
        grid_spec=pltpu.PrefetchScalarGridSpec(
            num_scalar_prefetch=1, grid=(tokens // tm,),
            in_specs=[row, pl.BlockSpec(memory_space=pl.ANY),
                      pl.BlockSpec((tm, 2), lambda i, s: (i, 0)), vec, vec],
            out_specs=specs,
            scratch_shapes=[pltpu.VMEM((2, 2, tm, d_model), F32),
                            pltpu.SemaphoreType.DMA((2,))]),
        compiler_params=_params(1), name="ln_moe",
    )(slot, h, ys, gates, gain.reshape(depth, 1, d_model), bias.reshape(depth, 1, d_model))


def _dispatch_kernel(slot_ref, zero_row_ref, zero_on_ref, x_ref, dst_ref, zero_sc, sems,
                     *, tokens, rows, n_zero):
    @pl.when(pl.program_id(0) == 0)
    def _zero_fill():
        zero_sc[...] = jnp.zeros(zero_sc.shape, zero_sc.dtype)
        span = zero_sc.shape[0]

        def zero_copy(e):
            start = pl.multiple_of(zero_row_ref[e], span)
            return pltpu.make_async_copy(zero_sc, dst_ref.at[pl.ds(start, span)], sems.at[0])

        for e in range(n_zero):
            @pl.when(zero_on_ref[e] > 0)
            def _start(e=e):
                zero_copy(e).start()
        for e in range(n_zero):
            @pl.when(zero_on_ref[e] > 0)
            def _wait(e=e):
                zero_copy(e).wait()

    base = pl.program_id(0) * rows

    def body(r, carry):
        for k in range(2):
            dst = slot_ref[k * tokens + base + r]
            pltpu.make_async_copy(x_ref.at[pl.ds(r, 1)], dst_ref.at[pl.ds(dst, 1)], sems.at[1]).start()
        return carry

    lax.fori_loop(0, rows, body, 0, unroll=4)
    for k in range(2):
        pltpu.make_async_copy(x_ref, dst_ref.at[pl.ds(0, rows)], sems.at[1]).wait()


def _dispatch(hp, slot, zero_rows, zero_on, *, n_slots, tm):
    tokens, width = hp.shape
    rows = _tile(tokens, 256)
    kern = functools.partial(_dispatch_kernel, tokens=tokens, rows=rows, n_zero=zero_rows.shape[0])
    return pl.pallas_call(
        kern,
        out_shape=jax.ShapeDtypeStruct((n_slots, width), hp.dtype),
        grid_spec=pltpu.PrefetchScalarGridSpec(
            num_scalar_prefetch=3, grid=(tokens // rows,),
            in_specs=[pl.BlockSpec((rows, width), lambda i, *_: (i, 0))],
            out_specs=pl.BlockSpec(memory_space=pl.ANY),
            scratch_shapes=[pltpu.VMEM((tm, width), hp.dtype),
                            pltpu.SemaphoreType.DMA((2,))]),
        compiler_params=_params(1),
        name="dispatch",
    )(slot, zero_rows, zero_on, hp)


def _cast_kernel(x_ref, o_ref):
    o_ref[...] = x_ref[...].astype(o_ref.dtype)


def _to_bf16(w):
    n, r, c = w.shape
    tr = _tile(r, 1024)
    return pl.pallas_call(
        _cast_kernel,
        out_shape=jax.ShapeDtypeStruct(w.shape, BF16),
        grid=(n, r // tr),
        in_specs=[pl.BlockSpec((1, tr, c), lambda a, b: (a, b, 0))],
        out_specs=pl.BlockSpec((1, tr, c), lambda a, b: (a, b, 0)),
        compiler_params=_params(2),
        name="to_bf16",
    )(w)


def _expert_kernel(tile_expert_ref, tile_block_ref, n_used_ref, x_ref, wg_ref, wu_ref, wd_ref, o_ref,
                   *, d_model):
    del tile_expert_ref, tile_block_ref
    half = d_model // 2

    @pl.when(pl.program_id(0) >= n_used_ref[0])
    def _unused_tile():
        o_ref[...] = jnp.zeros(o_ref.shape, o_ref.dtype)

    @pl.when(pl.program_id(0) < n_used_ref[0])
    def _compute():
        lo, hi = _unpack_bf16_pair(x_ref[...])
        gate = (jnp.dot(lo, wg_ref[0, :half, :], preferred_element_type=F32)
                + jnp.dot(hi, wg_ref[0, half:, :], preferred_element_type=F32))
        up = (jnp.dot(lo, wu_ref[0, :half, :], preferred_element_type=F32)
              + jnp.dot(hi, wu_ref[0, half:, :], preferred_element_type=F32))
        act = (gate * jax.nn.sigmoid(gate) * up).astype(BF16)
        col = _tile(d_model, 1024)
        for c in range(d_model // col):
            o_ref[:, c * col:(c + 1) * col] = jnp.dot(act, wd_ref[0, :, c * col:(c + 1) * col],
                                                      preferred_element_type=F32)


def _expert_mlp(xs, wg, wu, wd, tile_expert, tile_block, n_used, *, layer, n_experts, tm, n_tiles):
    d_model, d_ff = wg.shape[1], wg.shape[2]
    base = layer * n_experts
    return pl.pallas_call(
        functools.partial(_expert_kernel, d_model=d_model),
        out_shape=jax.ShapeDtypeStruct((n_tiles * tm, d_model), F32),
        grid_spec=pltpu.PrefetchScalarGridSpec(
            num_scalar_prefetch=3, grid=(n_tiles,),
            in_specs=[pl.BlockSpec((tm, d_model // 2), lambda i, te, tb, nu: (tb[i], 0)),
                      pl.BlockSpec((1, d_model, d_ff), lambda i, te, tb, nu: (base + te[i], 0, 0)),
                      pl.BlockSpec((1, d_model, d_ff), lambda i, te, tb, nu: (base + te[i], 0, 0)),
                      pl.BlockSpec((1, d_ff, d_model), lambda i, te, tb, nu: (base + te[i], 0, 0))],
            out_specs=pl.BlockSpec((tm, d_model), lambda i, te, tb, nu: (i, 0))),
        compiler_params=_params(1),
        name="expert_mlp",
    )(tile_expert, tile_block, n_used, xs, wg, wu, wd)


def _grouped_moe(hp, routing, wg, wu, wd, *, tokens, layer, tm):
    idx, gates, rank, counts = routing
    n_experts = counts.shape[0]

    counts = counts[:, 0]
    padded = ((counts + tm - 1) // tm) * tm
    seg_end = jnp.cumsum(padded)
    seg_start = seg_end - padded
    n_tiles = (2 * tokens) // tm + n_experts
    slot = (jnp.take(seg_start, idx.reshape(-1)) + rank.reshape(-1)).astype(jnp.int32)
    n_used = (seg_end[-1] // tm).astype(jnp.int32)
    tile_block = jnp.minimum(jnp.arange(n_tiles, dtype=jnp.int32), n_used - 1)
    tile_expert = jnp.sum(seg_end[None, :] <= (tile_block * tm)[:, None], axis=1).astype(jnp.int32)
    tile_expert = jnp.minimum(tile_expert, n_experts - 1)
    spare = n_used + jnp.arange(n_experts, dtype=jnp.int32)
    zero_rows = jnp.concatenate([jnp.maximum(seg_end - tm, 0).astype(jnp.int32),
                                 jnp.minimum(spare, n_tiles - 1) * tm])
    zero_on = jnp.concatenate([padded > counts, spare < n_tiles]).astype(jnp.int32)

    xs = _dispatch(hp, slot, zero_rows, zero_on, n_slots=n_tiles * tm, tm=tm)
    ys = _expert_mlp(xs, wg, wu, wd, tile_expert, tile_block, n_used.reshape(1),
                     layer=layer, n_experts=n_experts, tm=tm, n_tiles=n_tiles)
    return ys, slot, gates.T


def kernel(x, w_qkv_diff, w_o_diff, lambda_q1, lambda_k1, lambda_q2, lambda_k2, subln_g, w_pool,
           pool_scale, w_qkv_sb, w_o_sb, ln_mix_g, ln_mix_b, ln_ffn_g, ln_ffn_b, router_w,
           router_bias, w_gate, w_up, w_down):
    batch, seq, d_model = x.shape
    depth = ln_mix_g.shape[0]
    n_experts = router_w.shape[1]
    d_ff = w_gate.shape[-1]
    tokens = batch * seq
    alpha = (2.0 * depth) ** 0.25
    moe_tile = _tile(2 * tokens, 256)

    wg = _to_bf16(w_gate.reshape(depth * n_experts, d_model, d_ff))
    wu = _to_bf16(w_up.reshape(depth * n_experts, d_model, d_ff))
    wd = _to_bf16(w_down.reshape(depth * n_experts, d_ff, d_model))
    cos, sin = _rope_tables(seq)

    h = x.reshape(tokens, d_model)
    hb = None
    for i in range(depth):
        kind, slot = i % N_MIXERS, i // N_MIXERS
        if kind == 1:
            mix = _pool_mixer(h, w_pool, pool_scale, slot, seq=seq)
        else:
            if hb is None:
                hb = h.astype(BF16)
            if kind == 0:
                lambda_init = 0.8 - 0.6 * math.exp(-0.3 * i)
                qkv = _qkv_project(hb, w_qkv_diff, slot, cos, sin, seq=seq, rope=True)
                o = _diff_attention(qkv, lambda_q1[slot], lambda_k1[slot], lambda_q2[slot],
                                    lambda_k2[slot], subln_g[slot], batch=batch, seq=seq,
                                    d_model=d_model, lambda_init=lambda_init)
                mix = _project(o, w_o_diff, slot, F32)
            else:
                qkv = _qkv_project(hb, w_qkv_sb, slot, cos, sin, seq=seq, rope=False)
                o = _sb_attention(qkv, batch=batch, seq=seq, d_model=d_model)
                mix = _project(o, w_o_sb, slot, F32)
        h, hp, *routing = _ln_route(h, mix, ln_mix_g, ln_mix_b, i, router_w, router_bias, alpha=alpha)
        ys, slot_of, gates = _grouped_moe(hp, routing, wg, wu, wd, tokens=tokens, layer=i, tm=moe_tile)
        emit_bf16 = i + 1 < depth and (i + 1) % N_MIXERS != 1
        outs = _ln_moe(h, ys, slot_of, gates, ln_ffn_g, ln_ffn_b, i, alpha=alpha, emit_bf16=emit_bf16)
        h = outs[0]
        hb = outs[1] if emit_bf16 else None
    return h.reshape(batch, seq, d_model)
```

```python
import functools
import math

import jax
import jax.numpy as jnp
from jax import lax
from jax.experimental import pallas as pl
from jax.experimental.pallas import tpu as pltpu

F32 = jnp.float32
BF16 = jnp.bfloat16

HEAD_DIM = 128
CHUNK = 64
N_MIXERS = 3
POOL_WINDOWS = (2, 4, 8, 16)
POOL_HALO = 16
N_GROUPS = 4
ROPE_THETA = 10000.0
LN_EPS = 1e-5
RMS_EPS = 1e-5

LANES = 128
VMEM_LIMIT = 60 * 1024 * 1024
ROW_CHUNK = 256
ATTN_ROWS = 512
SB_KEYS = 256
EXP_UNDERFLOW = -104.0


def _tile(n, pref):
    return pref if n % pref == 0 else n


def _params(n_axes, vmem=VMEM_LIMIT):
    return pltpu.CompilerParams(dimension_semantics=("arbitrary",) * n_axes, vmem_limit_bytes=vmem)


def _proj_kernel(x_ref, w_ref, o_ref, wb_ref):
    @pl.when(pl.program_id(1) == 0)
    def _cast():
        wb_ref[...] = w_ref[0].astype(BF16)

    chunk = _tile(x_ref.shape[0], ROW_CHUNK)
    for c in range(x_ref.shape[0] // chunk):
        rows = slice(c * chunk, (c + 1) * chunk)
        o_ref[rows, :] = jnp.dot(x_ref[rows, :], wb_ref[...],
                                 preferred_element_type=F32).astype(o_ref.dtype)


def _project(x, w, slot, out_dtype):
    m, k = x.shape
    n = w.shape[2]
    tm, tn = _tile(m, 1024), _tile(n, 512)
    return pl.pallas_call(
        _proj_kernel,
        out_shape=jax.ShapeDtypeStruct((m, n), out_dtype),
        grid=(n // tn, m // tm),
        in_specs=[pl.BlockSpec((tm, k), lambda j, i: (i, 0)),
                  pl.BlockSpec((1, k, tn), lambda j, i: (slot, 0, j))],
        out_specs=pl.BlockSpec((tm, tn), lambda j, i: (i, j)),
        scratch_shapes=[pltpu.VMEM((k, tn), BF16)],
        compiler_params=_params(2),
        name="proj",
    )(x, w)


def _qkv_kernel(x_ref, w_ref, cos_ref, sin_ref, o_ref, wb_ref, *, d_model, tn, rope, scale):
    col0 = pl.program_id(0) * tn

    @pl.when(pl.program_id(1) == 0)
    def _cast():
        wb_ref[...] = w_ref[0].astype(BF16)

    q_scale = jnp.where(col0 < d_model, scale, 1.0).astype(F32)
    chunk = _tile(x_ref.shape[0], ROW_CHUNK)
    row_chunks = [slice(c * chunk, (c + 1) * chunk) for c in range(x_ref.shape[0] // chunk)]

    def product(rows):
        return jnp.dot(x_ref[rows, :], wb_ref[...], preferred_element_type=F32)

    def plain(scale_by):
        for rows in row_chunks:
            acc = product(rows)
            o_ref[rows, :] = (acc if scale_by is None else acc * scale_by).astype(o_ref.dtype)

    if rope:
        @pl.when(col0 < 2 * d_model)
        def _rotary():
            for rows in row_chunks:
                acc = product(rows)
                cos, sin = cos_ref[rows, :], sin_ref[rows, :]
                for c in range(tn // HEAD_DIM):
                    t = acc[:, c * HEAD_DIM:(c + 1) * HEAD_DIM]
                    r = (t * cos + pltpu.roll(t, HEAD_DIM // 2, axis=1) * sin) * q_scale
                    o_ref[rows, c * HEAD_DIM:(c + 1) * HEAD_DIM] = r.astype(o_ref.dtype)

        @pl.when(col0 >= 2 * d_model)
        def _values():
            plain(None)
    else:
        plain(q_scale)


def _qkv_project(x, w, slot, cos, sin, *, seq, rope):
    m, k = x.shape
    n = w.shape[2]
    tm, tn = _tile(seq, 1024), _tile(k, 512)
    n_pos = seq // tm
    kern = functools.partial(_qkv_kernel, d_model=k, tn=tn, rope=rope, scale=HEAD_DIM ** -0.5)
    return pl.pallas_call(
        kern,
        out_shape=jax.ShapeDtypeStruct((m, n), BF16),
        grid=(n // tn, m // tm),
        in_specs=[pl.BlockSpec((tm, k), lambda j, i: (i, 0)),
                  pl.BlockSpec((1, k, tn), lambda j, i: (slot, 0, j)),
                  pl.BlockSpec((tm, HEAD_DIM), lambda j, i: (i % n_pos, 0)),
                  pl.BlockSpec((tm, HEAD_DIM), lambda j, i: (i % n_pos, 0))],
        out_specs=pl.BlockSpec((tm, tn), lambda j, i: (i, j)),
        scratch_shapes=[pltpu.VMEM((k, tn), BF16)],
        compiler_params=_params(2),
        name="qkv_proj",
    )(x, w, cos, sin)


def _rope_tables(seq):
    half = HEAD_DIM // 2
    inv_freq = ROPE_THETA ** (-jnp.arange(half, dtype=F32) / half)
    ang = jnp.arange(seq).astype(F32)[:, None] * inv_freq[None, :]
    cos, sin = jnp.cos(ang), jnp.sin(ang)
    return jnp.concatenate([cos, cos], axis=-1), jnp.concatenate([-sin, sin], axis=-1)


def _nt_dot(a, b):
    return lax.dot_general(a, b, (((1,), (1,)), ((), ())), preferred_element_type=F32)


def _diff_attn_kernel(q1_ref, q2_ref, k1_ref, k2_ref, v_ref, lq1_ref, lk1_ref, lq2_ref, lk2_ref,
                      g_ref, o_ref, *, seq, t, lambda_init):
    lam = (jnp.exp(jnp.sum(lq1_ref[...] * lk1_ref[...], axis=-1, keepdims=True))
           - jnp.exp(jnp.sum(lq2_ref[...] * lk2_ref[...], axis=-1, keepdims=True)) + lambda_init)
    row_chunk = lax.broadcasted_iota(jnp.int32, (t, t), 0) // CHUNK
    col_chunk = lax.broadcasted_iota(jnp.int32, (t, t), 1) // CHUNK
    visible = col_chunk <= row_chunk

    for qi in range(seq // t):
        r0 = qi * t
        heads = []
        for q_ref, k_ref in ((q1_ref, k1_ref), (q2_ref, k2_ref)):
            q = q_ref[r0:r0 + t, :]
            s_diag = jnp.where(visible, _nt_dot(q, k_ref[r0:r0 + t, :]), -jnp.inf)
            m = jnp.max(s_diag, axis=-1, keepdims=True)
            if qi:
                s_past = _nt_dot(q, k_ref[0:r0, :])
                m = jnp.maximum(m, jnp.max(s_past, axis=-1, keepdims=True))
            p = jnp.exp(s_diag - m)
            denom = jnp.sum(p, axis=-1, keepdims=True)
            acc = jnp.dot(p.astype(BF16), v_ref[r0:r0 + t, :], preferred_element_type=F32)
            if qi:
                p = jnp.exp(s_past - m)
                denom = denom + jnp.sum(p, axis=-1, keepdims=True)
                acc = acc + jnp.dot(p.astype(BF16), v_ref[0:r0, :], preferred_element_type=F32)
            heads.append(acc / denom)
        o = heads[0] - lam * heads[1]
        o = o * lax.rsqrt(jnp.mean(o * o, axis=-1, keepdims=True) + RMS_EPS) * g_ref[...]
        o_ref[r0:r0 + t, :] = (o * (1.0 - lambda_init)).astype(o_ref.dtype)


def _diff_attention(qkv, lq1, lk1, lq2, lk2, subln_g, *, batch, seq, d_model, lambda_init):
    n_heads = d_model // (2 * HEAD_DIM)
    t = _tile(seq, ATTN_ROWS)
    kq = d_model // HEAD_DIM
    kv = 2 * d_model // (2 * HEAD_DIM)
    vec = lambda a: a.reshape(1, -1).astype(F32)
    small = lambda n: pl.BlockSpec((1, n), lambda b, h: (0, 0))
    kern = functools.partial(_diff_attn_kernel, seq=seq, t=t, lambda_init=lambda_init)
    return pl.pallas_call(
        kern,
        out_shape=jax.ShapeDtypeStruct((batch * seq, d_model), BF16),
        grid=(batch, n_heads),
        in_specs=[pl.BlockSpec((seq, HEAD_DIM), lambda b, h: (b, 2 * h)),
                  pl.BlockSpec((seq, HEAD_DIM), lambda b, h: (b, 2 * h + 1)),
                  pl.BlockSpec((seq, HEAD_DIM), lambda b, h: (b, kq + 2 * h)),
                  pl.BlockSpec((seq, HEAD_DIM), lambda b, h: (b, kq + 2 * h + 1)),
                  pl.BlockSpec((seq, 2 * HEAD_DIM), lambda b, h: (b, kv + h)),
                  small(HEAD_DIM), small(HEAD_DIM), small(HEAD_DIM), small(HEAD_DIM),
                  small(2 * HEAD_DIM)],
        out_specs=pl.BlockSpec((seq, 2 * HEAD_DIM), lambda b, h: (b, h)),
        compiler_params=_params(2),
        name="diff_attn",
    )(qkv, qkv, qkv, qkv, qkv, vec(lq1), vec(lk1), vec(lq2), vec(lk2), vec(subln_g))


def _sb_attn_kernel(q_ref, k_ref, v_ref, o_ref, *, seq, t, kb):
    src = lax.broadcasted_iota(jnp.int32, (kb, kb + LANES), 0)
    dst = lax.broadcasted_iota(jnp.int32, (kb, kb + LANES), 1)
    after = jnp.where((src > dst) | (dst >= kb), 1.0, 0.0).astype(BF16)
    row = lax.broadcasted_iota(jnp.int32, (t, kb), 0)
    col = lax.broadcasted_iota(jnp.int32, (t, kb), 1)

    def key_block(q, r0, j, acc, run):
        c0 = j * kb
        z = _nt_dot(q, k_ref[c0:c0 + kb, :])
        log1p_e = jnp.log(1.0 + jnp.exp(-jnp.abs(z)))
        log_keep = -(jnp.maximum(z, 0.0) + log1p_e)
        on_diagonal = c0 + kb > r0
        if on_diagonal:
            strict = col - row < r0 - c0
            log_keep = jnp.where(strict, log_keep, 0.0)
        hi = log_keep.astype(BF16)
        lo = (log_keep - hi.astype(F32)).astype(BF16)
        sums = (jnp.dot(hi, after, preferred_element_type=F32)
                + jnp.dot(lo, after, preferred_element_type=F32))
        later = sums[:, :kb] + jnp.concatenate([run] * (kb // LANES), axis=1)
        w = jnp.exp((jnp.minimum(z, 0.0) - log1p_e) + later)
        if on_diagonal:
            w = jnp.where(strict, w, 0.0)
        acc = acc + jnp.dot(w.astype(BF16), v_ref[c0:c0 + kb, :], preferred_element_type=F32)
        return acc, run + sums[:, kb:]

    for qi in range(seq // t):
        r0 = qi * t
        q = q_ref[r0:r0 + t, :]
        run = jnp.zeros((t, LANES), F32)
        acc = jnp.zeros((t, HEAD_DIM), F32)
        blocks = list(range((r0 + t) // kb - 1, -1, -1))
        n_near = t // kb + 1
        for j in blocks[:n_near]:
            acc, run = key_block(q, r0, j, acc, run)
        far = blocks[n_near:]
        if far:
            def far_blocks(q=q, r0=r0, far=far, acc=acc, run=run):
                for j in far:
                    acc, run = key_block(q, r0, j, acc, run)
                return acc

            acc = lax.cond(jnp.max(run) >= EXP_UNDERFLOW, far_blocks, lambda acc=acc: acc)
        o_ref[r0:r0 + t, :] = acc.astype(o_ref.dtype)


def _sb_attention(qkv, *, batch, seq, d_model):
    n_heads = d_model // HEAD_DIM
    t = _tile(seq, ATTN_ROWS)
    kb = _tile(t, SB_KEYS)
    blk = lambda first: pl.BlockSpec((seq, HEAD_DIM), lambda b, h: (b, first + h))
    return pl.pallas_call(
        functools.partial(_sb_attn_kernel, seq=seq, t=t, kb=kb),
        out_shape=jax.ShapeDtypeStruct((batch * seq, d_model), BF16),
        grid=(batch, n_heads),
        in_specs=[blk(0), blk(n_heads), blk(2 * n_heads)],
        out_specs=blk(0),
        compiler_params=_params(2),
        name="sb_attn",
    )(qkv, qkv, qkv)


def _pool_kernel(x_ref, halo_ref, w_ref, scale_ref, o_ref, wb_ref, *, tm, seq):
    g = pl.program_id(0)
    i = pl.program_id(1)

    @pl.when(i == 0)
    def _cast():
        wb_ref[...] = w_ref[0, 0].astype(BF16)

    pos0 = (i * tm) % seq
    x = x_ref[...]
    halo = halo_ref[...] * jnp.where(pos0 == 0, 0.0, 1.0).astype(F32)
    xe = jnp.concatenate([halo, x], axis=0)
    pos = pos0 + lax.broadcasted_iota(jnp.int32, (tm, 1), 0)

    for gi, window in enumerate(POOL_WINDOWS):
        @pl.when(g == gi)
        def _window(window=window):
            cur = xe
            span = 1
            while span < window:
                cur = cur + pltpu.roll(cur, span, axis=0)
                span *= 2
            count = jnp.minimum(pos + 1, window).astype(F32)
            pooled = cur[POOL_HALO:, :] / count - x
            y = jnp.dot(pooled.astype(BF16), wb_ref[...], preferred_element_type=F32)
            o_ref[...] = y * scale_ref[0]


def _pool_mixer(h, w_pool, pool_scale, slot, *, seq):
    tokens, d_model = h.shape
    groups, width = w_pool.shape[1], w_pool.shape[2]
    tm = _tile(seq, 512)
    halo_blocks = tm // POOL_HALO
    return pl.pallas_call(
        functools.partial(_pool_kernel, tm=tm, seq=seq),
        out_shape=jax.ShapeDtypeStruct((tokens, d_model), F32),
        grid=(groups, tokens // tm),
        in_specs=[pl.BlockSpec((tm, width), lambda g, i: (i, g)),
                  pl.BlockSpec((POOL_HALO, width), lambda g, i: (jnp.maximum(i * halo_blocks - 1, 0), g)),
                  pl.BlockSpec((1, 1, width, width), lambda g, i: (slot, g, 0, 0)),
                  pl.BlockSpec((1, 1, width), lambda g, i: (slot, 0, g))],
        out_specs=pl.BlockSpec((tm, width), lambda g, i: (i, g)),
        scratch_shapes=[pltpu.VMEM((width, width), BF16)],
        compiler_params=_params(2),
        name="pool_mixer",
    )(h, h, w_pool, pool_scale.reshape(pool_scale.shape[0], 1, d_model))


def _split_bf16(x):
    hi = x.astype(BF16)
    return hi, (x - hi.astype(F32)).astype(BF16)


def _first_argmax(vals):
    best = vals[0]
    for v in vals[1:]:
        best = jnp.maximum(best, v)
    idx = jnp.full(best.shape, len(vals) - 1, jnp.int32)
    for j in range(len(vals) - 2, -1, -1):
        idx = jnp.where(vals[j] == best, j, idx)
    return best, idx


def _route_tile(h, rw_ref, bias_ref, idx_ref, gate_ref, rank_ref, count_ref, run_sc):
    tm = h.shape[0]
    n_experts = rw_ref.shape[0]
    per_group = n_experts // N_GROUPS

    @pl.when(pl.program_id(0) == 0)
    def _init():
        run_sc[...] = jnp.zeros(run_sc.shape, F32)

    w_hi, w_lo = _split_bf16(rw_ref[...])
    h_hi, h_lo = _split_bf16(h)
    logits = _nt_dot(w_hi, h_hi) + (_nt_dot(w_hi, h_lo) + _nt_dot(w_lo, h_hi))
    aff = jax.nn.sigmoid(logits)
    sel = aff + bias_ref[...]
    rows = [sel[e:e + 1, :] for e in range(n_experts)]
    aff_rows = [aff[e:e + 1, :] for e in range(n_experts)]

    scores = []
    for g in range(N_GROUPS):
        a, b, c, d = rows[g * per_group:(g + 1) * per_group]
        lo1, hi1 = jnp.minimum(a, b), jnp.maximum(a, b)
        lo2, hi2 = jnp.minimum(c, d), jnp.maximum(c, d)
        scores.append(jnp.maximum(hi1, hi2) + jnp.maximum(jnp.minimum(hi1, hi2), jnp.maximum(lo1, lo2)))
    _, grp = _first_argmax(scores)

    def in_group(table, j):
        out = table[(N_GROUPS - 1) * per_group + j]
        for g in range(N_GROUPS - 2, -1, -1):
            out = jnp.where(grp == g, table[g * per_group + j], out)
        return out

    cand = [in_group(rows, j) for j in range(per_group)]
    cand_aff = [in_group(aff_rows, j) for j in range(per_group)]
    _, loc0 = _first_argmax(cand)
    rest = [jnp.where(loc0 == j, -jnp.inf, cand[j]) for j in range(per_group)]
    _, loc1 = _first_argmax(rest)

    def pick(table, loc):
        out = table[per_group - 1]
        for j in range(per_group - 2, -1, -1):
            out = jnp.where(loc == j, table[j], out)
        return out

    g0, g1 = pick(cand_aff, loc0), pick(cand_aff, loc1)
    total = g0 + g1
    e0 = grp * per_group + loc0
    e1 = grp * per_group + loc1
    idx_ref[0:1, :] = e0
    idx_ref[1:2, :] = e1
    gate_ref[0:1, :] = g0 / total
    gate_ref[1:2, :] = g1 / total

    expert = lax.broadcasted_iota(jnp.int32, (n_experts, tm), 0)
    hit0 = expert == e0
    hit1 = expert == e1
    chosen = jnp.where(hit0, 1.0, jnp.where(hit1, 1.0, 0.0))
    earlier = jnp.where(lax.broadcasted_iota(jnp.int32, (tm, tm), 0)
                        < lax.broadcasted_iota(jnp.int32, (tm, tm), 1), 1.0, 0.0).astype(BF16)
    before = jnp.dot(chosen.astype(BF16), earlier, preferred_element_type=F32) + run_sc[...]
    rank_ref[0:1, :] = jnp.sum(jnp.where(hit0, before, 0.0), axis=0, keepdims=True).astype(jnp.int32)
    rank_ref[1:2, :] = jnp.sum(jnp.where(hit1, before, 0.0), axis=0, keepdims=True).astype(jnp.int32)
    run = run_sc[...] + jnp.sum(chosen, axis=1, keepdims=True)
    run_sc[...] = run
    count_ref[...] = jnp.broadcast_to(run, count_ref.shape).astype(jnp.int32)


def _pack_bf16_pair(lo, hi):
    lo_bits = pltpu.bitcast(lo.astype(BF16).astype(F32), jnp.uint32) >> 16
    hi_bits = pltpu.bitcast(hi.astype(BF16).astype(F32), jnp.uint32) & jnp.uint32(0xFFFF0000)
    return lo_bits | hi_bits


def _unpack_bf16_pair(packed):
    lo = pltpu.bitcast(packed << 16, F32).astype(BF16)
    hi = pltpu.bitcast(packed & jnp.uint32(0xFFFF0000), F32).astype(BF16)
    return lo, hi


def _layer_norm_rows(y, g, b):
    mu = jnp.mean(y, axis=-1, keepdims=True)
    yc = y - mu
    var = jnp.mean(yc * yc, axis=-1, keepdims=True)
    return yc * lax.rsqrt(var + LN_EPS) * g + b


def _ln_route_kernel(h_ref, a_ref, g_ref, b_ref, rw_ref, bias_ref,
                     o_ref, packed_ref, idx_ref, gate_ref, rank_ref, count_ref, run_sc, *, alpha):
    y = alpha * h_ref[...] + a_ref[...]
    out = _layer_norm_rows(y, g_ref[0], b_ref[0])
    o_ref[...] = out
    half = out.shape[1] // 2
    packed_ref[...] = _pack_bf16_pair(out[:, :half], out[:, half:])
    _route_tile(out, rw_ref, bias_ref, idx_ref, gate_ref, rank_ref, count_ref, run_sc)


def _ln_route(h, mix, gain, bias, layer, router_w, router_bias, *, alpha):
    tokens, d_model = h.shape
    n_experts = router_w.shape[1]
    depth = gain.shape[0]
    tm = _tile(tokens, 256)
    half = d_model // 2
    row = pl.BlockSpec((tm, d_model), lambda i: (i, 0))
    vec = pl.BlockSpec((1, 1, d_model), lambda i: (layer, 0, 0))
    out2 = lambda dt: jax.ShapeDtypeStruct((2, tokens), dt)
    spec2 = pl.BlockSpec((2, tm), lambda i: (0, i))
    return pl.pallas_call(
        functools.partial(_ln_route_kernel, alpha=alpha),
        out_shape=[jax.ShapeDtypeStruct((tokens, d_model), F32),
                   jax.ShapeDtypeStruct((tokens, half), jnp.uint32),
                   out2(jnp.int32), out2(F32), out2(jnp.int32),
                   jax.ShapeDtypeStruct((n_experts, LANES), jnp.int32)],
        grid=(tokens // tm,),
        in_specs=[row, row, vec, vec,
                  pl.BlockSpec((n_experts, d_model), lambda i: (0, 0)),
                  pl.BlockSpec((n_experts, 1), lambda i: (0, 0))],
        out_specs=[row, pl.BlockSpec((tm, half), lambda i: (i, 0)),
                   spec2, spec2, spec2, pl.BlockSpec((n_experts, LANES), lambda i: (0, 0))],
        scratch_shapes=[pltpu.VMEM((n_experts, 1), F32)],
        compiler_params=_params(1), name="ln_route",
    )(h, mix, gain.reshape(depth, 1, d_model), bias.reshape(depth, 1, d_model),
      router_w.T, router_bias.reshape(n_experts, 1).astype(F32))


def _ln_moe_kernel(slot_ref, h_ref, ys_ref, gate_ref, g_ref, b_ref, *rest, alpha, tokens):
    *o_refs, ybuf, sems = rest
    rows, d_model = h_ref.shape
    i = pl.program_id(0)
    n_steps = pl.num_programs(0)

    def gather(step, buf):
        def body(r, carry):
            for k in range(2):
                src = slot_ref[k * tokens + step * rows + r]
                pltpu.make_async_copy(ys_ref.at[pl.ds(src, 1)], ybuf.at[buf, k, pl.ds(r, 1)],
                                      sems.at[buf]).start()
            return carry
        lax.fori_loop(0, rows, body, 0, unroll=4)

    @pl.when(i == 0)
    def _first():
        gather(0, 0)

    @pl.when(i + 1 < n_steps)
    def _ahead():
        gather(i + 1, (i + 1) % 2)

    buf = i % 2
    for k in range(2):
        pltpu.make_async_copy(ys_ref.at[pl.ds(0, rows)], ybuf.at[buf, k], sems.at[buf]).wait()

    gates = gate_ref[...]
    ffn = gates[:, 0:1] * ybuf[buf, 0] + gates[:, 1:2] * ybuf[buf, 1]
    y = alpha * h_ref[...] + ffn
    out = _layer_norm_rows(y, g_ref[0], b_ref[0])
    o_refs[0][...] = out
    if len(o_refs) > 1:
        o_refs[1][...] = out.astype(BF16)


def _ln_moe(h, ys, slot, gates, gain, bias, layer, *, alpha, emit_bf16):
    tokens, d_model = h.shape
    tm = _tile(tokens, 256)
    depth = gain.shape[0]
    row = pl.BlockSpec((tm, d_model), lambda i, s: (i, 0))
    vec = pl.BlockSpec((1, 1, d_model), lambda i, s: (layer, 0, 0))
    shapes = [jax.ShapeDtypeStruct((tokens, d_model), F32)]
    specs = [row]
    if emit_bf16:
        shapes.append(jax.ShapeDtypeStruct((tokens, d_model), BF16))
        specs.append(row)
    return pl.pallas_call(
        functools.partial(_ln_moe_kernel, alpha=alpha, tokens=tokens),
        out_shape=shapes,
        grid_spec=pltpu.PrefetchScalarGridSpec(
            num_scalar_prefetch=1, grid=(tokens // tm,),
            in_specs=[row, pl.BlockSpec(memory_space=pl.ANY),
                      pl.BlockSpec((tm, 2), lambda i, s: (i, 0)), vec, vec],
            out_specs=specs,
            scratch_shapes=[pltpu.VMEM((2, 2, tm, d_model), F32),
                            pltpu.SemaphoreType.DMA((2,))]),
        compiler_params=_params(1), name="ln_moe",
    )(slot, h, ys, gates, gain.reshape(depth, 1, d_model), bias.reshape(depth, 1, d_model))


def _dispatch_kernel(slot_ref, zero_row_ref, zero_on_ref, x_ref, dst_ref, zero_sc, sems,
                     *, tokens, rows, n_zero):
    @pl.when(pl.program_id(0) == 0)
    def _zero_fill():
        zero_sc[...] = jnp.zeros(zero_sc.shape, zero_sc.dtype)
        span = zero_sc.shape[0]

        def zero_copy(e):
            start = pl.multiple_of(zero_row_ref[e], span)
            return pltpu.make_async_copy(zero_sc, dst_ref.at[pl.ds(start, span)], sems.at[0])

        for e in range(n_zero):
            @pl.when(zero_on_ref[e] > 0)
            def _start(e=e):
                zero_copy(e).start()
        for e in range(n_zero):
            @pl.when(zero_on_ref[e] > 0)
            def _wait(e=e):
                zero_copy(e).wait()

    base = pl.program_id(0) * rows

    def body(r, carry):
        for k in range(2):
            dst = slot_ref[k * tokens + base + r]
            pltpu.make_async_copy(x_ref.at[pl.ds(r, 1)], dst_ref.at[pl.ds(dst, 1)], sems.at[1]).start()
        return carry

    lax.fori_loop(0, rows, body, 0, unroll=4)
    for k in range(2):
        pltpu.make_async_copy(x_ref, dst_ref.at[pl.ds(0, rows)], sems.at[1]).wait()


def _dispatch(hp, slot, zero_rows, zero_on, *, n_slots, tm):
    tokens, width = hp.shape
    rows = _tile(tokens, 256)
    kern = functools.partial(_dispatch_kernel, tokens=tokens, rows=rows, n_zero=zero_rows.shape[0])
    return pl.pallas_call(
        kern,
        out_shape=jax.ShapeDtypeStruct((n_slots, width), hp.dtype),
        grid_spec=pltpu.PrefetchScalarGridSpec(
            num_scalar_prefetch=3, grid=(tokens // rows,),
            in_specs=[pl.BlockSpec((rows, width), lambda i, *_: (i, 0))],
            out_specs=pl.BlockSpec(memory_space=pl.ANY),
            scratch_shapes=[pltpu.VMEM((tm, width), hp.dtype),
                            pltpu.SemaphoreType.DMA((2,))]),
        compiler_params=_params(1),
        name="dispatch",
    )(slot, zero_rows, zero_on, hp)


def _cast_kernel(x_ref, o_ref):
    o_ref[...] = x_ref[...].astype(o_ref.dtype)


def _to_bf16(w):
    n, r, c = w.shape
    tr = _tile(r, 1024)
    return pl.pallas_call(
        _cast_kernel,
        out_shape=jax.ShapeDtypeStruct(w.shape, BF16),
        grid=(n, r // tr),
        in_specs=[pl.BlockSpec((1, tr, c), lambda a, b: (a, b, 0))],
        out_specs=pl.BlockSpec((1, tr, c), lambda a, b: (a, b, 0)),
        compiler_params=_params(2),
        name="to_bf16",
    )(w)


def _expert_kernel(tile_expert_ref, tile_block_ref, n_used_ref, x_ref, wg_ref, wu_ref, wd_ref, o_ref,
                   *, d_model):
    del tile_expert_ref, tile_block_ref
    half = d_model // 2

    @pl.when(pl.program_id(0) >= n_used_ref[0])
    def _unused_tile():
        o_ref[...] = jnp.zeros(o_ref.shape, o_ref.dtype)

    @pl.when(pl.program_id(0) < n_used_ref[0])
    def _compute():
        lo, hi = _unpack_bf16_pair(x_ref[...])
        gate = (jnp.dot(lo, wg_ref[0, :half, :], preferred_element_type=F32)
                + jnp.dot(hi, wg_ref[0, half:, :], preferred_element_type=F32))
        up = (jnp.dot(lo, wu_ref[0, :half, :], preferred_element_type=F32)
              + jnp.dot(hi, wu_ref[0, half:, :], preferred_element_type=F32))
        act = (gate * jax.nn.sigmoid(gate) * up).astype(BF16)
        col = _tile(d_model, 1024)
        for c in range(d_model // col):
            o_ref[:, c * col:(c + 1) * col] = jnp.dot(act, wd_ref[0, :, c * col:(c + 1) * col],
                                                      preferred_element_type=F32)


def _expert_mlp(xs, wg, wu, wd, tile_expert, tile_block, n_used, *, layer, n_experts, tm, n_tiles):
    d_model, d_ff = wg.shape[1], wg.shape[2]
    base = layer * n_experts
    return pl.pallas_call(
        functools.partial(_expert_kernel, d_model=d_model),
        out_shape=jax.ShapeDtypeStruct((n_tiles * tm, d_model), F32),
        grid_spec=pltpu.PrefetchScalarGridSpec(
            num_scalar_prefetch=3, grid=(n_tiles,),
            in_specs=[pl.BlockSpec((tm, d_model // 2), lambda i, te, tb, nu: (tb[i], 0)),
                      pl.BlockSpec((1, d_model, d_ff), lambda i, te, tb, nu: (base + te[i], 0, 0)),
                      pl.BlockSpec((1, d_model, d_ff), lambda i, te, tb, nu: (base + te[i], 0, 0)),
                      pl.BlockSpec((1, d_ff, d_model), lambda i, te, tb, nu: (base + te[i], 0, 0))],
            out_specs=pl.BlockSpec((tm, d_model), lambda i, te, tb, nu: (i, 0))),
        compiler_params=_params(1),
        name="expert_mlp",
    )(tile_expert, tile_block, n_used, xs, wg, wu, wd)


def _grouped_moe(hp, routing, wg, wu, wd, *, tokens, layer, tm):
    idx, gates, rank, counts = routing
    n_experts = counts.shape[0]

    counts = counts[:, 0]
    padded = ((counts + tm - 1) // tm) * tm
    seg_end = jnp.cumsum(padded)
    seg_start = seg_end - padded
    n_tiles = (2 * tokens) // tm + n_experts
    slot = (jnp.take(seg_start, idx.reshape(-1)) + rank.reshape(-1)).astype(jnp.int32)
    n_used = (seg_end[-1] // tm).astype(jnp.int32)
    tile_block = jnp.minimum(jnp.arange(n_tiles, dtype=jnp.int32), n_used - 1)
    tile_expert = jnp.sum(seg_end[None, :] <= (tile_block * tm)[:, None], axis=1).astype(jnp.int32)
    tile_expert = jnp.minimum(tile_expert, n_experts - 1)
    spare = n_used + jnp.arange(n_experts, dtype=jnp.int32)
    zero_rows = jnp.concatenate([jnp.maximum(seg_end - tm, 0).astype(jnp.int32),
                                 jnp.minimum(spare, n_tiles - 1) * tm])
    zero_on = jnp.concatenate([padded > counts, spare < n_tiles]).astype(jnp.int32)

    xs = _dispatch(hp, slot, zero_rows, zero_on, n_slots=n_tiles * tm, tm=tm)
    ys = _expert_mlp(xs, wg, wu, wd, tile_expert, tile_block, n_used.reshape(1),
                     layer=layer, n_experts=n_experts, tm=tm, n_tiles=n_tiles)
    return ys, slot, gates.T


def kernel(x, w_qkv_diff, w_o_diff, lambda_q1, lambda_k1, lambda_q2, lambda_k2, subln_g, w_pool,
           pool_scale, w_qkv_sb, w_o_sb, ln_mix_g, ln_mix_b, ln_ffn_g, ln_ffn_b, router_w,
           router_bias, w_gate, w_up, w_down):
    batch, seq, d_model = x.shape
    depth = ln_mix_g.shape[0]
    n_experts = router_w.shape[1]
    d_ff = w_gate.shape[-1]
    tokens = batch * seq
    alpha = (2.0 * depth) ** 0.25
    moe_tile = _tile(2 * tokens, 256)

    wg = _to_bf16(w_gate.reshape(depth * n_experts, d_model, d_ff))
    wu = _to_bf16(w_up.reshape(depth * n_experts, d_model, d_ff))
    wd = _to_bf16(w_down.reshape(depth * n_experts, d_ff, d_model))
    cos, sin = _rope_tables(seq)

    h = x.reshape(tokens, d_model)
    hb = None
    for i in range(depth):
        kind, slot = i % N_MIXERS, i // N_MIXERS
        if kind == 1:
            mix = _pool_mixer(h, w_pool, pool_scale, slot, seq=seq)
        else:
            if hb is None:
                hb = h.astype(BF16)
            if kind == 0:
                lambda_init = 0.8 - 0.6 * math.exp(-0.3 * i)
                qkv = _qkv_project(hb, w_qkv_diff, slot, cos, sin, seq=seq, rope=True)
                o = _diff_attention(qkv, lambda_q1[slot], lambda_k1[slot], lambda_q2[slot],
                                    lambda_k2[slot], subln_g[slot], batch=batch, seq=seq,
                                    d_model=d_model, lambda_init=lambda_init)
                mix = _project(o, w_o_diff, slot, F32)
            else:
                qkv = _qkv_project(hb, w_qkv_sb, slot, cos, sin, seq=seq, rope=False)
                o = _sb_attention(qkv, batch=batch, seq=seq, d_model=d_model)
                mix = _project(o, w_o_sb, slot, F32)
        h, hp, *routing = _ln_route(h, mix, ln_mix_g, ln_mix_b, i, router_w, router_bias, alpha=alpha)
        ys, slot_of, gates = _grouped_moe(hp, routing, wg, wu, wd, tokens=tokens, layer=i, tm=moe_tile)
        emit_bf16 = i + 1 < depth and (i + 1) % N_MIXERS != 1
        outs = _ln_moe(h, ys, slot_of, gates, ln_ffn_g, ln_ffn_b, i, alpha=alpha, emit_bf16=emit_bf16)
        h = outs[0]
        hb = outs[1] if emit_bf16 else None
    return h.reshape(batch, seq, d_model)
```

```python
import functools
import math

import jax
import jax.numpy as jnp
from jax import lax
from jax.experimental import pallas as pl
from jax.experimental.pallas import tpu as pltpu

F32 = jnp.float32
BF16 = jnp.bfloat16

HEAD_DIM = 128
CHUNK = 64
N_MIXERS = 3
POOL_WINDOWS = (2, 4, 8, 16)
POOL_HALO = 16
N_GROUPS = 4
ROPE_THETA = 10000.0
LN_EPS = 1e-5
RMS_EPS = 1e-5

LANES = 128
VMEM_LIMIT = 60 * 1024 * 1024
ROW_CHUNK = 256
ATTN_ROWS = 512
SB_KEYS = 256
EXP_UNDERFLOW = -104.0


def _tile(n, pref):
    return pref if n % pref == 0 else n


def _params(n_axes, vmem=VMEM_LIMIT):
    return pltpu.CompilerParams(dimension_semantics=("arbitrary",) * n_axes, vmem_limit_bytes=vmem)


def _proj_kernel(x_ref, w_ref, o_ref, wb_ref):
    @pl.when(pl.program_id(1) == 0)
    def _cast():
        wb_ref[...] = w_ref[0].astype(BF16)

    chunk = _tile(x_ref.shape[0], ROW_CHUNK)
    for c in range(x_ref.shape[0] // chunk):
        rows = slice(c * chunk, (c + 1) * chunk)
        o_ref[rows, :] = jnp.dot(x_ref[rows, :], wb_ref[...],
                                 preferred_element_type=F32).astype(o_ref.dtype)


def _project(x, w, slot, out_dtype):
    m, k = x.shape
    n = w.shape[2]
    tm, tn = _tile(m, 1024), _tile(n, 512)
    return pl.pallas_call(
        _proj_kernel,
        out_shape=jax.ShapeDtypeStruct((m, n), out_dtype),
        grid=(n // tn, m // tm),
        in_specs=[pl.BlockSpec((tm, k), lambda j, i: (i, 0)),
                  pl.BlockSpec((1, k, tn), lambda j, i: (slot, 0, j))],
        out_specs=pl.BlockSpec((tm, tn), lambda j, i: (i, j)),
        scratch_shapes=[pltpu.VMEM((k, tn), BF16)],
        compiler_params=_params(2),
        name="proj",
    )(x, w)


def _qkv_kernel(x_ref, w_ref, cos_ref, sin_ref, o_ref, wb_ref, *, d_model, tn, rope, scale):
    col0 = pl.program_id(0) * tn

    @pl.when(pl.program_id(1) == 0)
    def _cast():
        wb_ref[...] = w_ref[0].astype(BF16)

    q_scale = jnp.where(col0 < d_model, scale, 1.0).astype(F32)
    chunk = _tile(x_ref.shape[0], ROW_CHUNK)
    row_chunks = [slice(c * chunk, (c + 1) * chunk) for c in range(x_ref.shape[0] // chunk)]

    def product(rows):
        return jnp.dot(x_ref[rows, :], wb_ref[...], preferred_element_type=F32)

    def plain(scale_by):
        for rows in row_chunks:
            acc = product(rows)
            o_ref[rows, :] = (acc if scale_by is None else acc * scale_by).astype(o_ref.dtype)

    if rope:
        @pl.when(col0 < 2 * d_model)
        def _rotary():
            for rows in row_chunks:
                acc = product(rows)
                cos, sin = cos_ref[rows, :], sin_ref[rows, :]
                for c in range(tn // HEAD_DIM):
                    t = acc[:, c * HEAD_DIM:(c + 1) * HEAD_DIM]
                    r = (t * cos + pltpu.roll(t, HEAD_DIM // 2, axis=1) * sin) * q_scale
                    o_ref[rows, c * HEAD_DIM:(c + 1) * HEAD_DIM] = r.astype(o_ref.dtype)

        @pl.when(col0 >= 2 * d_model)
        def _values():
            plain(None)
    else:
        plain(q_scale)


def _qkv_project(x, w, slot, cos, sin, *, seq, rope):
    m, k = x.shape
    n = w.shape[2]
    tm, tn = _tile(seq, 1024), _tile(k, 512)
    n_pos = seq // tm
    kern = functools.partial(_qkv_kernel, d_model=k, tn=tn, rope=rope, scale=HEAD_DIM ** -0.5)
    return pl.pallas_call(
        kern,
        out_shape=jax.ShapeDtypeStruct((m, n), BF16),
        grid=(n // tn, m // tm),
        in_specs=[pl.BlockSpec((tm, k), lambda j, i: (i, 0)),
                  pl.BlockSpec((1, k, tn), lambda j, i: (slot, 0, j)),
                  pl.BlockSpec((tm, HEAD_DIM), lambda j, i: (i % n_pos, 0)),
                  pl.BlockSpec((tm, HEAD_DIM), lambda j, i: (i % n_pos, 0))],
        out_specs=pl.BlockSpec((tm, tn), lambda j, i: (i, j)),
        scratch_shapes=[pltpu.VMEM((k, tn), BF16)],
        compiler_params=_params(2),
        name="qkv_proj",
    )(x, w, cos, sin)


def _rope_tables(seq):
    half = HEAD_DIM // 2
    inv_freq = ROPE_THETA ** (-jnp.arange(half, dtype=F32) / half)
    ang = jnp.arange(seq).astype(F32)[:, None] * inv_freq[None, :]
    cos, sin = jnp.cos(ang), jnp.sin(ang)
    return jnp.concatenate([cos, cos], axis=-1), jnp.concatenate([-sin, sin], axis=-1)


def _nt_dot(a, b):
    return lax.dot_general(a, b, (((1,), (1,)), ((), ())), preferred_element_type=F32)


def _diff_attn_kernel(q1_ref, q2_ref, k1_ref, k2_ref, v_ref, lq1_ref, lk1_ref, lq2_ref, lk2_ref,
                      g_ref, o_ref, *, seq, t, lambda_init):
    lam = (jnp.exp(jnp.sum(lq1_ref[...] * lk1_ref[...], axis=-1, keepdims=True))
           - jnp.exp(jnp.sum(lq2_ref[...] * lk2_ref[...], axis=-1, keepdims=True)) + lambda_init)
    row_chunk = lax.broadcasted_iota(jnp.int32, (t, t), 0) // CHUNK
    col_chunk = lax.broadcasted_iota(jnp.int32, (t, t), 1) // CHUNK
    visible = col_chunk <= row_chunk

    for qi in range(seq // t):
        r0 = qi * t
        heads = []
        for q_ref, k_ref in ((q1_ref, k1_ref), (q2_ref, k2_ref)):
            q = q_ref[r0:r0 + t, :]
            s_diag = jnp.where(visible, _nt_dot(q, k_ref[r0:r0 + t, :]), -jnp.inf)
            m = jnp.max(s_diag, axis=-1, keepdims=True)
            if qi:
                s_past = _nt_dot(q, k_ref[0:r0, :])
                m = jnp.maximum(m, jnp.max(s_past, axis=-1, keepdims=True))
            p = jnp.exp(s_diag - m)
            denom = jnp.sum(p, axis=-1, keepdims=True)
            acc = jnp.dot(p.astype(BF16), v_ref[r0:r0 + t, :], preferred_element_type=F32)
            if qi:
                p = jnp.exp(s_past - m)
                denom = denom + jnp.sum(p, axis=-1, keepdims=True)
                acc = acc + jnp.dot(p.astype(BF16), v_ref[0:r0, :], preferred_element_type=F32)
            heads.append(acc / denom)
        o = heads[0] - lam * heads[1]
        o = o * lax.rsqrt(jnp.mean(o * o, axis=-1, keepdims=True) + RMS_EPS) * g_ref[...]
        o_ref[r0:r0 + t, :] = (o * (1.0 - lambda_init)).astype(o_ref.dtype)


def _diff_attention(qkv, lq1, lk1, lq2, lk2, subln_g, *, batch, seq, d_model, lambda_init):
    n_heads = d_model // (2 * HEAD_DIM)
    t = _tile(seq, ATTN_ROWS)
    kq = d_model // HEAD_DIM
    kv = 2 * d_model // (2 * HEAD_DIM)
    vec = lambda a: a.reshape(1, -1).astype(F32)
    small = lambda n: pl.BlockSpec((1, n), lambda b, h: (0, 0))
    kern = functools.partial(_diff_attn_kernel, seq=seq, t=t, lambda_init=lambda_init)
    return pl.pallas_call(
        kern,
        out_shape=jax.ShapeDtypeStruct((batch * seq, d_model), BF16),
        grid=(batch, n_heads),
        in_specs=[pl.BlockSpec((seq, HEAD_DIM), lambda b, h: (b, 2 * h)),
                  pl.BlockSpec((seq, HEAD_DIM), lambda b, h: (b, 2 * h + 1)),
                  pl.BlockSpec((seq, HEAD_DIM), lambda b, h: (b, kq + 2 * h)),
                  pl.BlockSpec((seq, HEAD_DIM), lambda b, h: (b, kq + 2 * h + 1)),
                  pl.BlockSpec((seq, 2 * HEAD_DIM), lambda b, h: (b, kv + h)),
                  small(HEAD_DIM), small(HEAD_DIM), small(HEAD_DIM), small(HEAD_DIM),
                  small(2 * HEAD_DIM)],
        out_specs=pl.BlockSpec((seq, 2 * HEAD_DIM), lambda b, h: (b, h)),
        compiler_params=_params(2),
        name="diff_attn",
    )(qkv, qkv, qkv, qkv, qkv, vec(lq1), vec(lk1), vec(lq2), vec(lk2), vec(subln_g))


def _sb_attn_kernel(q_ref, k_ref, v_ref, o_ref, *, seq, t, kb):
    src = lax.broadcasted_iota(jnp.int32, (kb, kb + LANES), 0)
    dst = lax.broadcasted_iota(jnp.int32, (kb, kb + LANES), 1)
    after = jnp.where((src > dst) | (dst >= kb), 1.0, 0.0).astype(BF16)
    row = lax.broadcasted_iota(jnp.int32, (t, kb), 0)
    col = lax.broadcasted_iota(jnp.int32, (t, kb), 1)

    def key_block(q, r0, j, acc, run):
        c0 = j * kb
        z = _nt_dot(q, k_ref[c0:c0 + kb, :])
        log1p_e = jnp.log(1.0 + jnp.exp(-jnp.abs(z)))
        log_keep = -(jnp.maximum(z, 0.0) + log1p_e)
        on_diagonal = c0 + kb > r0
        if on_diagonal:
            strict = col - row < r0 - c0
            log_keep = jnp.where(strict, log_keep, 0.0)
        hi = log_keep.astype(BF16)
        lo = (log_keep - hi.astype(F32)).astype(BF16)
        sums = (jnp.dot(hi, after, preferred_element_type=F32)
                + jnp.dot(lo, after, preferred_element_type=F32))
        later = sums[:, :kb] + jnp.concatenate([run] * (kb // LANES), axis=1)
        w = jnp.exp((jnp.minimum(z, 0.0) - log1p_e) + later)
        if on_diagonal:
            w = jnp.where(strict, w, 0.0)
        acc = acc + jnp.dot(w.astype(BF16), v_ref[c0:c0 + kb, :], preferred_element_type=F32)
        return acc, run + sums[:, kb:]

    for qi in range(seq // t):
        r0 = qi * t
        q = q_ref[r0:r0 + t, :]
        run = jnp.zeros((t, LANES), F32)
        acc = jnp.zeros((t, HEAD_DIM), F32)
        blocks = list(range((r0 + t) // kb - 1, -1, -1))
        n_near = t // kb + 1
        for j in blocks[:n_near]:
            acc, run = key_block(q, r0, j, acc, run)
        far = blocks[n_near:]
        if far:
            def far_blocks(q=q, r0=r0, far=far, acc=acc, run=run):
                for j in far:
                    acc, run = key_block(q, r0, j, acc, run)
                return acc

            acc = lax.cond(jnp.max(run) >= EXP_UNDERFLOW, far_blocks, lambda acc=acc: acc)
        o_ref[r0:r0 + t, :] = acc.astype(o_ref.dtype)


def _sb_attention(qkv, *, batch, seq, d_model):
    n_heads = d_model // HEAD_DIM
    t = _tile(seq, ATTN_ROWS)
    kb = _tile(t, SB_KEYS)
    blk = lambda first: pl.BlockSpec((seq, HEAD_DIM), lambda b, h: (b, first + h))
    return pl.pallas_call(
        functools.partial(_sb_attn_kernel, seq=seq, t=t, kb=kb),
        out_shape=jax.ShapeDtypeStruct((batch * seq, d_model), BF16),
        grid=(batch, n_heads),
        in_specs=[blk(0), blk(n_heads), blk(2 * n_heads)],
        out_specs=blk(0),
        compiler_params=_params(2),
        name="sb_attn",
    )(qkv, qkv, qkv)


def _pool_kernel(x_ref, halo_ref, w_ref, scale_ref, o_ref, wb_ref, *, tm, seq):
    g = pl.program_id(0)
    i = pl.program_id(1)

    @pl.when(i == 0)
    def _cast():
        wb_ref[...] = w_ref[0, 0].astype(BF16)

    pos0 = (i * tm) % seq
    x = x_ref[...]
    halo = halo_ref[...] * jnp.where(pos0 == 0, 0.0, 1.0).astype(F32)
    xe = jnp.concatenate([halo, x], axis=0)
    pos = pos0 + lax.broadcasted_iota(jnp.int32, (tm, 1), 0)

    for gi, window in enumerate(POOL_WINDOWS):
        @pl.when(g == gi)
        def _window(window=window):
            cur = xe
            span = 1
            while span < window:
                cur = cur + pltpu.roll(cur, span, axis=0)
                span *= 2
            count = jnp.minimum(pos + 1, window).astype(F32)
            pooled = cur[POOL_HALO:, :] / count - x
            y = jnp.dot(pooled.astype(BF16), wb_ref[...], preferred_element_type=F32)
            o_ref[...] = y * scale_ref[0]


def _pool_mixer(h, w_pool, pool_scale, slot, *, seq):
    tokens, d_model = h.shape
    groups, width = w_pool.shape[1], w_pool.shape[2]
    tm = _tile(seq, 512)
    halo_blocks = tm // POOL_HALO
    return pl.pallas_call(
        functools.partial(_pool_kernel, tm=tm, seq=seq),
        out_shape=jax.ShapeDtypeStruct((tokens, d_model), F32),
        grid=(groups, tokens // tm),
        in_specs=[pl.BlockSpec((tm, width), lambda g, i: (i, g)),
                  pl.BlockSpec((POOL_HALO, width), lambda g, i: (jnp.maximum(i * halo_blocks - 1, 0), g)),
                  pl.BlockSpec((1, 1, width, width), lambda g, i: (slot, g, 0, 0)),
                  pl.BlockSpec((1, 1, width), lambda g, i: (slot, 0, g))],
        out_specs=pl.BlockSpec((tm, width), lambda g, i: (i, g)),
        scratch_shapes=[pltpu.VMEM((width, width), BF16)],
        compiler_params=_params(2),
        name="pool_mixer",
    )(h, h, w_pool, pool_scale.reshape(pool_scale.shape[0], 1, d_model))


def _split_bf16(x):
    hi = x.astype(BF16)
    return hi, (x - hi.astype(F32)).astype(BF16)


def _first_argmax(vals):
    best = vals[0]
    for v in vals[1:]:
        best = jnp.maximum(best, v)
    idx = jnp.full(best.shape, len(vals) - 1, jnp.int32)
    for j in range(len(vals) - 2, -1, -1):
        idx = jnp.where(vals[j] == best, j, idx)
    return best, idx


def _route_tile(h, rw_ref, bias_ref, idx_ref, gate_ref, rank_ref, count_ref, run_sc):
    tm = h.shape[0]
    n_experts = rw_ref.shape[0]
    per_group = n_experts // N_GROUPS

    @pl.when(pl.program_id(0) == 0)
    def _init():
        run_sc[...] = jnp.zeros(run_sc.shape, F32)

    w_hi, w_lo = _split_bf16(rw_ref[...])
    h_hi, h_lo = _split_bf16(h)
    logits = _nt_dot(w_hi, h_hi) + (_nt_dot(w_hi, h_lo) + _nt_dot(w_lo, h_hi))
    aff = jax.nn.sigmoid(logits)
    sel = aff + bias_ref[...]
    rows = [sel[e:e + 1, :] for e in range(n_experts)]
    aff_rows = [aff[e:e + 1, :] for e in range(n_experts)]

    scores = []
    for g in range(N_GROUPS):
        a, b, c, d = rows[g * per_group:(g + 1) * per_group]
        lo1, hi1 = jnp.minimum(a, b), jnp.maximum(a, b)
        lo2, hi2 = jnp.minimum(c, d), jnp.maximum(c, d)
        scores.append(jnp.maximum(hi1, hi2) + jnp.maximum(jnp.minimum(hi1, hi2), jnp.maximum(lo1, lo2)))
    _, grp = _first_argmax(scores)

    def in_group(table, j):
        out = table[(N_GROUPS - 1) * per_group + j]
        for g in range(N_GROUPS - 2, -1, -1):
            out = jnp.where(grp == g, table[g * per_group + j], out)
        return out

    cand = [in_group(rows, j) for j in range(per_group)]
    cand_aff = [in_group(aff_rows, j) for j in range(per_group)]
    _, loc0 = _first_argmax(cand)
    rest = [jnp.where(loc0 == j, -jnp.inf, cand[j]) for j in range(per_group)]
    _, loc1 = _first_argmax(rest)

    def pick(table, loc):
        out = table[per_group - 1]
        for j in range(per_group - 2, -1, -1):
            out = jnp.where(loc == j, table[j], out)
        return out

    g0, g1 = pick(cand_aff, loc0), pick(cand_aff, loc1)
    total = g0 + g1
    e0 = grp * per_group + loc0
    e1 = grp * per_group + loc1
    idx_ref[0:1, :] = e0
    idx_ref[1:2, :] = e1
    gate_ref[0:1, :] = g0 / total
    gate_ref[1:2, :] = g1 / total

    expert = lax.broadcasted_iota(jnp.int32, (n_experts, tm), 0)
    hit0 = expert == e0
    hit1 = expert == e1
    chosen = jnp.where(hit0, 1.0, jnp.where(hit1, 1.0, 0.0))
    earlier = jnp.where(lax.broadcasted_iota(jnp.int32, (tm, tm), 0)
                        < lax.broadcasted_iota(jnp.int32, (tm, tm), 1), 1.0, 0.0).astype(BF16)
    before = jnp.dot(chosen.astype(BF16), earlier, preferred_element_type=F32) + run_sc[...]
    rank_ref[0:1, :] = jnp.sum(jnp.where(hit0, before, 0.0), axis=0, keepdims=True).astype(jnp.int32)
    rank_ref[1:2, :] = jnp.sum(jnp.where(hit1, before, 0.0), axis=0, keepdims=True).astype(jnp.int32)
    run = run_sc[...] + jnp.sum(chosen, axis=1, keepdims=True)
    run_sc[...] = run
    count_ref[...] = jnp.broadcast_to(run, count_ref.shape).astype(jnp.int32)


def _pack_bf16_pair(lo, hi):
    lo_bits = pltpu.bitcast(lo.astype(BF16).astype(F32), jnp.uint32) >> 16
    hi_bits = pltpu.bitcast(hi.astype(BF16).astype(F32), jnp.uint32) & jnp.uint32(0xFFFF0000)
    return lo_bits | hi_bits


def _unpack_bf16_pair(packed):
    lo = pltpu.bitcast(packed << 16, F32).astype(BF16)
    hi = pltpu.bitcast(packed & jnp.uint32(0xFFFF0000), F32).astype(BF16)
    return lo, hi


def _unpack_f32_pair(packed):
    return (pltpu.bitcast(packed << 16, F32),
            pltpu.bitcast(packed & jnp.uint32(0xFFFF0000), F32))


def _layer_norm_rows(y, g, b):
    mu = jnp.mean(y, axis=-1, keepdims=True)
    yc = y - mu
    var = jnp.mean(yc * yc, axis=-1, keepdims=True)
    return yc * lax.rsqrt(var + LN_EPS) * g + b


def _ln_route_kernel(h_ref, a_ref, g_ref, b_ref, rw_ref, bias_ref,
                     o_ref, packed_ref, idx_ref, gate_ref, rank_ref, count_ref, run_sc, *, alpha):
    y = alpha * h_ref[...] + a_ref[...]
    out = _layer_norm_rows(y, g_ref[0], b_ref[0])
    o_ref[...] = out
    half = out.shape[1] // 2
    packed_ref[...] = _pack_bf16_pair(out[:, :half], out[:, half:])
    _route_tile(out, rw_ref, bias_ref, idx_ref, gate_ref, rank_ref, count_ref, run_sc)


def _ln_route(h, mix, gain, bias, layer, router_w, router_bias, *, alpha):
    tokens, d_model = h.shape
    n_experts = router_w.shape[1]
    depth = gain.shape[0]
    tm = _tile(tokens, 256)
    half = d_model // 2
    row = pl.BlockSpec((tm, d_model), lambda i: (i, 0))
    vec = pl.BlockSpec((1, 1, d_model), lambda i: (layer, 0, 0))
    out2 = lambda dt: jax.ShapeDtypeStruct((2, tokens), dt)
    spec2 = pl.BlockSpec((2, tm), lambda i: (0, i))
    return pl.pallas_call(
        functools.partial(_ln_route_kernel, alpha=alpha),
        out_shape=[jax.ShapeDtypeStruct((tokens, d_model), F32),
                   jax.ShapeDtypeStruct((tokens, half), jnp.uint32),
                   out2(jnp.int32), out2(F32), out2(jnp.int32),
                   jax.ShapeDtypeStruct((n_experts, LANES), jnp.int32)],
        grid=(tokens // tm,),
        in_specs=[row, row, vec, vec,
                  pl.BlockSpec((n_experts, d_model), lambda i: (0, 0)),
                  pl.BlockSpec((n_experts, 1), lambda i: (0, 0))],
        out_specs=[row, pl.BlockSpec((tm, half), lambda i: (i, 0)),
                   spec2, spec2, spec2, pl.BlockSpec((n_experts, LANES), lambda i: (0, 0))],
        scratch_shapes=[pltpu.VMEM((n_experts, 1), F32)],
        compiler_params=_params(1), name="ln_route",
    )(h, mix, gain.reshape(depth, 1, d_model), bias.reshape(depth, 1, d_model),
      router_w.T, router_bias.reshape(n_experts, 1).astype(F32))


def _ln_moe_kernel(slot_ref, h_ref, ys_ref, gate_ref, g_ref, b_ref, *rest, alpha, tokens):
    *o_refs, ybuf, sems = rest
    rows, d_model = h_ref.shape
    i = pl.program_id(0)
    n_steps = pl.num_programs(0)

    def gather(step, buf):
        def body(r, carry):
            for k in range(2):
                src = slot_ref[k * tokens + step * rows + r]
                pltpu.make_async_copy(ys_ref.at[pl.ds(src, 1)], ybuf.at[buf, k, pl.ds(r, 1)],
                                      sems.at[buf]).start()
            return carry
        lax.fori_loop(0, rows, body, 0, unroll=4)

    @pl.when(i == 0)
    def _first():
        gather(0, 0)

    @pl.when(i + 1 < n_steps)
    def _ahead():
        gather(i + 1, (i + 1) % 2)

    buf = i % 2
    for k in range(2):
        pltpu.make_async_copy(ys_ref.at[pl.ds(0, rows)], ybuf.at[buf, k], sems.at[buf]).wait()

    gates = gate_ref[...]
    lo0, hi0 = _unpack_f32_pair(ybuf[buf, 0])
    lo1, hi1 = _unpack_f32_pair(ybuf[buf, 1])
    ffn = jnp.concatenate([gates[:, 0:1] * lo0 + gates[:, 1:2] * lo1,
                           gates[:, 0:1] * hi0 + gates[:, 1:2] * hi1], axis=1)
    y = alpha * h_ref[...] + ffn
    out = _layer_norm_rows(y, g_ref[0], b_ref[0])
    o_refs[0][...] = out
    if len(o_refs) > 1:
        o_refs[1][...] = out.astype(BF16)


def _ln_moe(h, ys, slot, gates, gain, bias, layer, *, alpha, emit_bf16):
    tokens, d_model = h.shape
    tm = _tile(tokens, 256)
    depth = gain.shape[0]
    row = pl.BlockSpec((tm, d_model), lambda i, s: (i, 0))
    vec = pl.BlockSpec((1, 1, d_model), lambda i, s: (layer, 0, 0))
    shapes = [jax.ShapeDtypeStruct((tokens, d_model), F32)]
    specs = [row]
    if emit_bf16:
        shapes.append(jax.ShapeDtypeStruct((tokens, d_model), BF16))
        specs.append(row)
    return pl.pallas_call(
        functools.partial(_ln_moe_kernel, alpha=alpha, tokens=tokens),
        out_shape=shapes,
        grid_spec=pltpu.PrefetchScalarGridSpec(
            num_scalar_prefetch=1, grid=(tokens // tm,),
            in_specs=[row, pl.BlockSpec(memory_space=pl.ANY),
                      pl.BlockSpec((tm, 2), lambda i, s: (i, 0)), vec, vec],
            out_specs=specs,
            scratch_shapes=[pltpu.VMEM((2, 2, tm, d_model // 2), jnp.uint32),
                            pltpu.SemaphoreType.DMA((2,))]),
        compiler_params=_params(1), name="ln_moe",
    )(slot, h, ys, gates, gain.reshape(depth, 1, d_model), bias.reshape(depth, 1, d_model))


def _dispatch_kernel(slot_ref, zero_row_ref, zero_on_ref, x_ref, dst_ref, zero_sc, sems,
                     *, tokens, rows, n_zero):
    @pl.when(pl.program_id(0) == 0)
    def _zero_fill():
        zero_sc[...] = jnp.zeros(zero_sc.shape, zero_sc.dtype)
        span = zero_sc.shape[0]

        def zero_copy(e):
            start = pl.multiple_of(zero_row_ref[e], span)
            return pltpu.make_async_copy(zero_sc, dst_ref.at[pl.ds(start, span)], sems.at[0])

        for e in range(n_zero):
            @pl.when(zero_on_ref[e] > 0)
            def _start(e=e):
                zero_copy(e).start()
        for e in range(n_zero):
            @pl.when(zero_on_ref[e] > 0)
            def _wait(e=e):
                zero_copy(e).wait()

    base = pl.program_id(0) * rows

    def body(r, carry):
        for k in range(2):
            dst = slot_ref[k * tokens + base + r]
            pltpu.make_async_copy(x_ref.at[pl.ds(r, 1)], dst_ref.at[pl.ds(dst, 1)], sems.at[1]).start()
        return carry

    lax.fori_loop(0, rows, body, 0, unroll=4)
    for k in range(2):
        pltpu.make_async_copy(x_ref, dst_ref.at[pl.ds(0, rows)], sems.at[1]).wait()


def _dispatch(hp, slot, zero_rows, zero_on, *, n_slots, tm):
    tokens, width = hp.shape
    rows = _tile(tokens, 256)
    kern = functools.partial(_dispatch_kernel, tokens=tokens, rows=rows, n_zero=zero_rows.shape[0])
    return pl.pallas_call(
        kern,
        out_shape=jax.ShapeDtypeStruct((n_slots, width), hp.dtype),
        grid_spec=pltpu.PrefetchScalarGridSpec(
            num_scalar_prefetch=3, grid=(tokens // rows,),
            in_specs=[pl.BlockSpec((rows, width), lambda i, *_: (i, 0))],
            out_specs=pl.BlockSpec(memory_space=pl.ANY),
            scratch_shapes=[pltpu.VMEM((tm, width), hp.dtype),
                            pltpu.SemaphoreType.DMA((2,))]),
        compiler_params=_params(1),
        name="dispatch",
    )(slot, zero_rows, zero_on, hp)


def _cast_kernel(x_ref, o_ref):
    o_ref[...] = x_ref[...].astype(o_ref.dtype)


def _to_bf16(w):
    n, r, c = w.shape
    tr = _tile(r, 1024)
    return pl.pallas_call(
        _cast_kernel,
        out_shape=jax.ShapeDtypeStruct(w.shape, BF16),
        grid=(n, r // tr),
        in_specs=[pl.BlockSpec((1, tr, c), lambda a, b: (a, b, 0))],
        out_specs=pl.BlockSpec((1, tr, c), lambda a, b: (a, b, 0)),
        compiler_params=_params(2),
        name="to_bf16",
    )(w)


def _expert_kernel(tile_expert_ref, tile_block_ref, n_used_ref, x_ref, wg_ref, wu_ref, wd_ref, o_ref,
                   *, d_model):
    del tile_expert_ref, tile_block_ref
    half = d_model // 2

    @pl.when(pl.program_id(0) >= n_used_ref[0])
    def _unused_tile():
        o_ref[...] = jnp.zeros(o_ref.shape, o_ref.dtype)

    @pl.when(pl.program_id(0) < n_used_ref[0])
    def _compute():
        lo, hi = _unpack_bf16_pair(x_ref[...])
        gate = (jnp.dot(lo, wg_ref[0, :half, :], preferred_element_type=F32)
                + jnp.dot(hi, wg_ref[0, half:, :], preferred_element_type=F32))
        up = (jnp.dot(lo, wu_ref[0, :half, :], preferred_element_type=F32)
              + jnp.dot(hi, wu_ref[0, half:, :], preferred_element_type=F32))
        act = (gate * jax.nn.sigmoid(gate) * up).astype(BF16)
        col = _tile(half, 1024)
        for c in range(half // col):
            lo = jnp.dot(act, wd_ref[0, :, c * col:(c + 1) * col], preferred_element_type=F32)
            hi = jnp.dot(act, wd_ref[0, :, half + c * col:half + (c + 1) * col],
                         preferred_element_type=F32)
            o_ref[:, c * col:(c + 1) * col] = _pack_bf16_pair(lo, hi)


def _expert_mlp(xs, wg, wu, wd, tile_expert, tile_block, n_used, *, layer, n_experts, tm, n_tiles):
    d_model, d_ff = wg.shape[1], wg.shape[2]
    base = layer * n_experts
    return pl.pallas_call(
        functools.partial(_expert_kernel, d_model=d_model),
        out_shape=jax.ShapeDtypeStruct((n_tiles * tm, d_model // 2), jnp.uint32),
        grid_spec=pltpu.PrefetchScalarGridSpec(
            num_scalar_prefetch=3, grid=(n_tiles,),
            in_specs=[pl.BlockSpec((tm, d_model // 2), lambda i, te, tb, nu: (tb[i], 0)),
                      pl.BlockSpec((1, d_model, d_ff), lambda i, te, tb, nu: (base + te[i], 0, 0)),
                      pl.BlockSpec((1, d_model, d_ff), lambda i, te, tb, nu: (base + te[i], 0, 0)),
                      pl.BlockSpec((1, d_ff, d_model), lambda i, te, tb, nu: (base + te[i], 0, 0))],
            out_specs=pl.BlockSpec((tm, d_model // 2), lambda i, te, tb, nu: (i, 0))),
        compiler_params=_params(1),
        name="expert_mlp",
    )(tile_expert, tile_block, n_used, xs, wg, wu, wd)


def _grouped_moe(hp, routing, wg, wu, wd, *, tokens, layer, tm):
    idx, gates, rank, counts = routing
    n_experts = counts.shape[0]

    counts = counts[:, 0]
    padded = ((counts + tm - 1) // tm) * tm
    seg_end = jnp.cumsum(padded)
    seg_start = seg_end - padded
    n_tiles = (2 * tokens) // tm + n_experts
    slot = (jnp.take(seg_start, idx.reshape(-1)) + rank.reshape(-1)).astype(jnp.int32)
    n_used = (seg_end[-1] // tm).astype(jnp.int32)
    tile_block = jnp.minimum(jnp.arange(n_tiles, dtype=jnp.int32), n_used - 1)
    tile_expert = jnp.sum(seg_end[None, :] <= (tile_block * tm)[:, None], axis=1).astype(jnp.int32)
    tile_expert = jnp.minimum(tile_expert, n_experts - 1)
    spare = n_used + jnp.arange(n_experts, dtype=jnp.int32)
    zero_rows = jnp.concatenate([jnp.maximum(seg_end - tm, 0).astype(jnp.int32),
                                 jnp.minimum(spare, n_tiles - 1) * tm])
    zero_on = jnp.concatenate([padded > counts, spare < n_tiles]).astype(jnp.int32)

    xs = _dispatch(hp, slot, zero_rows, zero_on, n_slots=n_tiles * tm, tm=tm)
    ys = _expert_mlp(xs, wg, wu, wd, tile_expert, tile_block, n_used.reshape(1),
                     layer=layer, n_experts=n_experts, tm=tm, n_tiles=n_tiles)
    return ys, slot, gates.T


def kernel(x, w_qkv_diff, w_o_diff, lambda_q1, lambda_k1, lambda_q2, lambda_k2, subln_g, w_pool,
           pool_scale, w_qkv_sb, w_o_sb, ln_mix_g, ln_mix_b, ln_ffn_g, ln_ffn_b, router_w,
           router_bias, w_gate, w_up, w_down):
    batch, seq, d_model = x.shape
    depth = ln_mix_g.shape[0]
    n_experts = router_w.shape[1]
    d_ff = w_gate.shape[-1]
    tokens = batch * seq
    alpha = (2.0 * depth) ** 0.25
    moe_tile = _tile(2 * tokens, 256)

    wg = _to_bf16(w_gate.reshape(depth * n_experts, d_model, d_ff))
    wu = _to_bf16(w_up.reshape(depth * n_experts, d_model, d_ff))
    wd = _to_bf16(w_down.reshape(depth * n_experts, d_ff, d_model))
    cos, sin = _rope_tables(seq)

    h = x.reshape(tokens, d_model)
    hb = None
    for i in range(depth):
        kind, slot = i % N_MIXERS, i // N_MIXERS
        if kind == 1:
            mix = _pool_mixer(h, w_pool, pool_scale, slot, seq=seq)
        else:
            if hb is None:
                hb = h.astype(BF16)
            if kind == 0:
                lambda_init = 0.8 - 0.6 * math.exp(-0.3 * i)
                qkv = _qkv_project(hb, w_qkv_diff, slot, cos, sin, seq=seq, rope=True)
                o = _diff_attention(qkv, lambda_q1[slot], lambda_k1[slot], lambda_q2[slot],
                                    lambda_k2[slot], subln_g[slot], batch=batch, seq=seq,
                                    d_model=d_model, lambda_init=lambda_init)
                mix = _project(o, w_o_diff, slot, F32)
            else:
                qkv = _qkv_project(hb, w_qkv_sb, slot, cos, sin, seq=seq, rope=False)
                o = _sb_attention(qkv, batch=batch, seq=seq, d_model=d_model)
                mix = _project(o, w_o_sb, slot, F32)
        h, hp, *routing = _ln_route(h, mix, ln_mix_g, ln_mix_b, i, router_w, router_bias, alpha=alpha)
        ys, slot_of, gates = _grouped_moe(hp, routing, wg, wu, wd, tokens=tokens, layer=i, tm=moe_tile)
        emit_bf16 = i + 1 < depth and (i + 1) % N_MIXERS != 1
        outs = _ln_moe(h, ys, slot_of, gates, ln_ffn_g, ln_ffn_b, i, alpha=alpha, emit_bf16=emit_bf16)
        h = outs[0]
        hb = outs[1] if emit_bf16 else None
    return h.reshape(batch, seq, d_model)
```
